```python
import jax, jax.numpy as jnp
from jax import lax
import numpy as np

D_MODEL = 1024
BATCH = 16
SEQ = 4096
DEPTH = 2
DEC_BATCH = 16
DEC_SEQ = 2048
PAST_LEN = 128

A_GROUPS = ((128, 1), (512, 4), (2048, 16))
A_HEADS = 8
A_HEAD_DIM = 64
A_WIDTH = A_HEADS * A_HEAD_DIM
A_QKV_W = len(A_GROUPS) * A_WIDTH
B_HEADS = 4
B_DK = 128
B_DV = 256
B_KW = B_HEADS * B_DK
B_VW = B_HEADS * B_DV
B_GATE_RANK = 16
B_GATE_TAU = 16.0
B_CHUNK = 64
D_FF = 4 * D_MODEL
EPS = 1e-6
NEG = -1e30

IN_SPLITS = [A_QKV_W, A_QKV_W, A_QKV_W,
             B_KW, B_KW, B_VW, B_VW,
             B_GATE_RANK, B_GATE_RANK,
             D_MODEL, D_MODEL]
IN_COLS = int(sum(IN_SPLITS))
IN_SPLIT_POINTS = [int(c) for c in np.cumsum(IN_SPLITS)[:-1]]

kernel_name = "hybrid_dilated_gla_encoder"


def rmsnorm(x, g):
    xf = x.astype(jnp.float32)
    y = xf * lax.rsqrt(jnp.mean(jnp.square(xf), axis=-1, keepdims=True) + EPS)
    return (y * g.astype(jnp.float32)).astype(x.dtype)


def alibi_slopes(n):
    return jnp.exp2(-8.0 * jnp.arange(1, n + 1, dtype=jnp.float32) / n)


def dilated_window_attention(q, k, v, slopes, window, dilation):
    bsz, seq, nh, hd = q.shape
    half = window // (2 * dilation)
    blk = half
    sub = seq // dilation
    nb = -(-sub // blk)
    subp = nb * blk
    grp = bsz * dilation

    def to_sub(t):
        return t.reshape(bsz, sub, dilation, nh, hd).transpose(0, 2, 1, 3, 4).reshape(grp, sub, nh, hd)

    qs = jnp.pad(to_sub(q), ((0, 0), (0, subp - sub), (0, 0), (0, 0))).reshape(grp, nb, blk, nh, hd)

    def key_windows(t):
        tp = jnp.pad(to_sub(t), ((0, 0), (blk, subp - sub + blk), (0, 0), (0, 0)))
        tp = tp.reshape(grp, nb + 2, blk, nh, hd)
        return jnp.concatenate([tp[:, :-2], tp[:, 1:-1], tp[:, 2:]], axis=2)

    kw = key_windows(k)
    vw = key_windows(v).astype(jnp.float32)
    qi = jnp.arange(blk)[:, None]
    ci = jnp.arange(3 * blk)[None, :]
    delta = ci - blk - qi
    jk = (jnp.arange(nb)[:, None, None] - 1) * blk + ci[None]
    valid = (jnp.abs(delta) <= half)[None] & (jk >= 0) & (jk < sub)
    dist = (jnp.abs(delta) * dilation).astype(jnp.float32)
    bias = -slopes[:, None, None] * dist[None]
    s = jnp.einsum('gnqhe,gnkhe->gnhqk', qs, kw, preferred_element_type=jnp.float32) * (hd ** -0.5)
    s = jnp.where(valid[None, :, None], s + bias[None, None], NEG)
    m = jnp.max(s, axis=-1, keepdims=True)
    p = jnp.exp(s - m)
    l = jnp.sum(p, axis=-1, keepdims=True)
    o = jnp.einsum('gnhqk,gnkhe->gnqhe', p, vw)
    o = o / jnp.moveaxis(l[..., 0], 2, 3)[..., None]
    lse = jnp.moveaxis((m + jnp.log(l))[..., 0], 2, 3)
    o = o.reshape(grp, subp, nh, hd)[:, :sub].reshape(bsz, dilation, sub, nh, hd)
    o = o.transpose(0, 2, 1, 3, 4).reshape(bsz, seq, nh, hd)
    lse = lse.reshape(grp, subp, nh)[:, :sub].reshape(bsz, dilation, sub, nh)
    lse = lse.transpose(0, 2, 1, 3).reshape(bsz, seq, nh)
    return o, lse


def mixer_a(qa, ka, va):
    bsz, seq, _ = qa.shape
    ng = len(A_GROUPS)
    slopes = alibi_slopes(ng * A_HEADS)
    shp = (bsz, seq, ng, A_HEADS, A_HEAD_DIM)
    qa, ka, va = qa.reshape(shp), ka.reshape(shp), va.reshape(shp)
    outs, lses = [], []
    for g, (win, dil) in enumerate(A_GROUPS):
        o, lse = dilated_window_attention(qa[:, :, g], ka[:, :, g], va[:, :, g],
                                          slopes[g * A_HEADS:(g + 1) * A_HEADS], win, dil)
        outs.append(o)
        lses.append(lse)
    wgt = jax.nn.softmax(jnp.stack(lses, axis=0), axis=0)
    o = jnp.sum(jnp.stack(outs, axis=0) * wgt[..., None], axis=0)
    return o.reshape(bsz, seq, A_WIDTH).astype(qa.dtype)


def gla_direction(q, k, v, log_a, strict):
    bsz, seq, nh, dk = q.shape
    dv = v.shape[-1]
    c = B_CHUNK
    n = seq // c
    q = q.reshape(bsz, n, c, nh, dk)
    k = k.reshape(bsz, n, c, nh, dk)
    v = v.reshape(bsz, n, c, nh, dv)
    b = jnp.cumsum(log_a.reshape(bsz, n, c, nh, dk), axis=2)
    b_last = b[:, :, -1:]
    qe = q * jnp.exp(b)
    ke = k * jnp.exp(-b)
    kd = k * jnp.exp(b_last - b)
    att = jnp.einsum('bnchk,bnshk->bnhcs', qe, ke)
    mask = jnp.tril(jnp.ones((c, c), dtype=bool), k=-1 if strict else 0)
    att = jnp.where(mask, att, 0.0)
    o_intra = jnp.einsum('bnhcs,bnshv->bnchv', att, v)

    def step(state, xs):
        qe_n, kd_n, v_n, dec_n = xs
        o_n = jnp.einsum('bchk,bhkv->bchv', qe_n, state)
        state = state * dec_n[..., None] + jnp.einsum('bchk,bchv->bhkv', kd_n, v_n)
        return state, o_n

    xs = (jnp.moveaxis(qe, 1, 0), jnp.moveaxis(kd, 1, 0), jnp.moveaxis(v, 1, 0),
          jnp.moveaxis(jnp.exp(b_last[:, :, 0]), 1, 0))
    _, o_inter = lax.scan(step, jnp.zeros((bsz, nh, dk, dv), jnp.float32), xs)
    o = o_intra + jnp.moveaxis(o_inter, 0, 1)
    return o.reshape(bsz, seq, nh, dv)


def mixer_b(qb, kb, vb, rb, glf, glb, w_gate_f, b_gate_f, w_gate_b, b_gate_b, norm_g):
    bsz, seq, _ = qb.shape
    f32 = jnp.float32
    q = qb.astype(f32).reshape(bsz, seq, B_HEADS, B_DK) * (B_DK ** -0.5)
    k = kb.astype(f32).reshape(bsz, seq, B_HEADS, B_DK)
    v = vb.astype(f32).reshape(bsz, seq, B_HEADS, B_DV)

    def log_decay(lr, w, bias):
        z = jnp.einsum('bsr,rk->bsk', lr.astype(f32), w.astype(f32)) + bias.astype(f32)
        return (jax.nn.log_sigmoid(z) / B_GATE_TAU).reshape(bsz, seq, B_HEADS, B_DK)

    def flip(t):
        return jnp.flip(t, axis=1)

    o_f = gla_direction(q, k, v, log_decay(glf, w_gate_f, b_gate_f), strict=False)
    o_b = flip(gla_direction(flip(q), flip(k), flip(v), flip(log_decay(glb, w_gate_b, b_gate_b)), strict=True))
    o = o_f + o_b
    o = o * lax.rsqrt(jnp.mean(jnp.square(o), axis=-1, keepdims=True) + EPS) * norm_g.astype(f32)
    o = o.reshape(bsz, seq, B_VW) * jax.nn.silu(rb.astype(f32))
    return o.astype(qb.dtype)


def encoder_layer(x, norm_mix_g, w_in, w_gate_f, b_gate_f, w_gate_b, b_gate_b, gla_norm_g,
                  w_branch_a, w_branch_b, w_out, norm_ffn_g, w_ff1, w_ff2):
    h = rmsnorm(x, norm_mix_g)
    proj = jnp.einsum('bsd,dc->bsc', h, w_in)
    qa, ka, va, qb, kb, vb, rb, glf, glb, ga, gb = jnp.split(proj, IN_SPLIT_POINTS, axis=-1)
    ya = jnp.einsum('bsc,cd->bsd', mixer_a(qa, ka, va), w_branch_a)
    yb = jnp.einsum('bsc,cd->bsd', mixer_b(qb, kb, vb, rb, glf, glb, w_gate_f, b_gate_f,
                                           w_gate_b, b_gate_b, gla_norm_g), w_branch_b)
    merged = jax.nn.sigmoid(ga) * ya + jax.nn.sigmoid(gb) * yb
    x = x + jnp.einsum('bsd,de->bse', merged, w_out)
    h = rmsnorm(x, norm_ffn_g)
    hid = jnp.square(jax.nn.relu(jnp.einsum('bsd,df->bsf', h, w_ff1)))
    return x + jnp.einsum('bsf,fd->bsd', hid, w_ff2)


def trunk(x, norm_mix_g, w_in, gla_w_gate_fwd, gla_b_gate_fwd, gla_w_gate_bwd, gla_b_gate_bwd,
          gla_norm_g, w_branch_a, w_branch_b, w_out, norm_ffn_g, w_ff1, w_ff2, final_norm_g):
    for l in range(DEPTH):
        x = encoder_layer(x, norm_mix_g[l], w_in[l], gla_w_gate_fwd[l], gla_b_gate_fwd[l],
                          gla_w_gate_bwd[l], gla_b_gate_bwd[l], gla_norm_g[l], w_branch_a[l],
                          w_branch_b[l], w_out[l], norm_ffn_g[l], w_ff1[l], w_ff2[l])
    return rmsnorm(x, final_norm_g)


def setup_inputs(seed: int = 0) -> dict:
    key = jax.random.key(seed)
    ks = jax.random.split(key, 16)
    f32 = jnp.float32

    def nrm(k, shape, scale):
        return jax.random.normal(k, shape, f32) * scale

    return {
        "x_prompt": nrm(ks[0], (BATCH, SEQ, D_MODEL), 1.0),
        "x_sample": nrm(ks[1], (DEC_BATCH, DEC_SEQ, D_MODEL), 1.0),
        "norm_mix_g": 1.0 + nrm(ks[2], (DEPTH, D_MODEL), 0.02),
        "w_in": nrm(ks[3], (DEPTH, D_MODEL, IN_COLS), D_MODEL ** -0.5),
        "gla_w_gate_fwd": nrm(ks[4], (DEPTH, B_GATE_RANK, B_KW), B_GATE_RANK ** -0.5),
        "gla_b_gate_fwd": nrm(ks[5], (DEPTH, B_KW), 0.1),
        "gla_w_gate_bwd": nrm(ks[6], (DEPTH, B_GATE_RANK, B_KW), B_GATE_RANK ** -0.5),
        "gla_b_gate_bwd": nrm(ks[7], (DEPTH, B_KW), 0.1),
        "gla_norm_g": 1.0 + nrm(ks[8], (DEPTH, B_DV), 0.02),
        "w_branch_a": nrm(ks[9], (DEPTH, A_WIDTH, D_MODEL), A_WIDTH ** -0.5),
        "w_branch_b": nrm(ks[10], (DEPTH, B_VW, D_MODEL), B_VW ** -0.5),
        "w_out": nrm(ks[11], (DEPTH, D_MODEL, D_MODEL), D_MODEL ** -0.5),
        "norm_ffn_g": 1.0 + nrm(ks[12], (DEPTH, D_MODEL), 0.02),
        "w_ff1": nrm(ks[13], (DEPTH, D_MODEL, D_FF), D_MODEL ** -0.5),
        "w_ff2": nrm(ks[14], (DEPTH, D_FF, D_MODEL), D_FF ** -0.5),
        "final_norm_g": 1.0 + nrm(ks[15], (D_MODEL,), 0.02),
    }


def reference(x_prompt, x_sample, norm_mix_g, w_in, gla_w_gate_fwd, gla_b_gate_fwd,
              gla_w_gate_bwd, gla_b_gate_bwd, gla_norm_g, w_branch_a, w_branch_b, w_out,
              norm_ffn_g, w_ff1, w_ff2, final_norm_g):
    y_prompt = trunk(x_prompt, norm_mix_g, w_in, gla_w_gate_fwd, gla_b_gate_fwd, gla_w_gate_bwd,
                     gla_b_gate_bwd, gla_norm_g, w_branch_a, w_branch_b, w_out, norm_ffn_g,
                     w_ff1, w_ff2, final_norm_g)
    y_sample = trunk(x_sample, norm_mix_g, w_in, gla_w_gate_fwd, gla_b_gate_fwd, gla_w_gate_bwd,
                     gla_b_gate_bwd, gla_norm_g, w_branch_a, w_branch_b, w_out, norm_ffn_g,
                     w_ff1, w_ff2, final_norm_g)
    return (y_prompt, y_sample)
```

```python
import functools

import jax
import jax.numpy as jnp
from jax import lax
from jax.experimental import pallas as pl
from jax.experimental.pallas import tpu as pltpu

F32 = jnp.float32
BF16 = jnp.bfloat16

D_MODEL = 1024
DEPTH = 2
A_GROUPS = ((128, 1), (512, 4), (2048, 16))
A_HEADS = 8
A_HEAD_DIM = 64
A_WIDTH = A_HEADS * A_HEAD_DIM
A_QKV_W = len(A_GROUPS) * A_WIDTH
A_HALF = 64
B_HEADS = 4
B_DK = 128
B_DV = 256
B_KW = B_HEADS * B_DK
B_VW = B_HEADS * B_DV
B_GATE_RANK = 16
B_GATE_TAU = 16.0
B_CHUNK = 64
D_FF = 4 * D_MODEL
EPS = 1e-6
NEG = -1e30

COL_QA = 0
COL_KA = COL_QA + A_QKV_W
COL_VA = COL_KA + A_QKV_W
COL_QB = COL_VA + A_QKV_W
COL_KB = COL_QB + B_KW
COL_VB = COL_KB + B_KW
COL_RB = COL_VB + B_VW
COL_GL = COL_RB + B_VW
GL_PAD = 512
COL_GA = COL_GL + GL_PAD
COL_GB = COL_GA + D_MODEL
PROJ_W = COL_GB + D_MODEL

VMEM_LIMIT_BYTES = 56 * 1024 * 1024

IN_TM = 1024
IN_TN = 2048
IN_SUB = 512
ROW_TM = 512
ATT_TQ = 512
ATT_SB = 128
ATT_KW = ATT_SB + 2 * A_HALF


def _cparams(*sem):
    return pltpu.CompilerParams(dimension_semantics=sem, vmem_limit_bytes=VMEM_LIMIT_BYTES)


def _resident(shape):
    nd = len(shape)
    return pl.BlockSpec(shape, lambda *_: (0,) * nd, pipeline_mode=pl.Buffered(1))


def _rms(x, g):
    return x * lax.rsqrt(jnp.mean(x * x, axis=-1, keepdims=True) + EPS) * g


def _sigmoid(t):
    return 1.0 / (1.0 + jnp.exp(-t))


def _inproj_kernel(x_ref, g_ref, w_ref, o_ref, h_ref):
    @pl.when(pl.program_id(1) == 0)
    def _():
        h_ref[...] = _rms(x_ref[...], g_ref[...]).astype(BF16)

    h = h_ref[...]
    for c in range(IN_TN // IN_SUB):
        sl = slice(c * IN_SUB, (c + 1) * IN_SUB)
        o_ref[:, sl] = jnp.dot(h, w_ref[:, sl], preferred_element_type=F32).astype(BF16)


def _inproj(x, g, w):
    t = x.shape[0]
    return pl.pallas_call(
        _inproj_kernel,
        grid=(t // IN_TM, PROJ_W // IN_TN),
        in_specs=[
            pl.BlockSpec((IN_TM, D_MODEL), lambda i, j: (i, 0)),
            pl.BlockSpec((1, D_MODEL), lambda i, j: (0, 0)),
            pl.BlockSpec((D_MODEL, IN_TN), lambda i, j: (0, j)),
        ],
        out_specs=pl.BlockSpec((IN_TM, IN_TN), lambda i, j: (i, j)),
        out_shape=jax.ShapeDtypeStruct((t, PROJ_W), BF16),
        scratch_shapes=[pltpu.VMEM((IN_TM, D_MODEL), BF16)],
        compiler_params=_cparams("parallel", "arbitrary"),
        name="inproj",
    )(x, g, w)


def _attn_kernel(q_ref, kp_ref, k_ref, kn_ref, vp_ref, v_ref, vn_ref, bm_ref, o_ref, st_ref,
                 kext, vext, *, tq, sub):
    i = pl.program_id(2)
    kext[0:A_HALF] = kp_ref[0]
    kext[A_HALF:A_HALF + tq] = k_ref[0]
    kext[A_HALF + tq:] = kn_ref[0]
    vext[0:A_HALF] = vp_ref[0]
    vext[A_HALF:A_HALF + tq] = v_ref[0]
    vext[A_HALF + tq:] = vn_ref[0]

    lane = lax.broadcasted_iota(jnp.int32, (ATT_SB, 128), 1)
    low = lane < A_HEAD_DIM
    col = lax.broadcasted_iota(jnp.int32, (ATT_SB, ATT_KW), 1)
    scale = A_HEAD_DIM ** -0.5

    def sub_block(jq, carry):
        r0 = pl.multiple_of(jq * ATT_SB, ATT_SB)
        kidx = i * tq + jq * ATT_SB - A_HALF + col
        rmask = jnp.where((kidx >= 0) & (kidx < sub), 0.0, NEG)
        st = jnp.zeros((ATT_SB, 128), F32)
        for p in range(A_HEADS // 2):
            ls = slice(p * 128, (p + 1) * 128)
            qp = q_ref[0, pl.ds(r0, ATT_SB), ls] * scale
            kw = kext[pl.ds(r0, ATT_KW), ls]
            vw = vext[pl.ds(r0, ATT_KW), ls]
            outs = []
            for hh in range(2):
                h = 2 * p + hh
                qm = jnp.where(low if hh == 0 else jnp.logical_not(low), qp, jnp.zeros_like(qp))
                s = lax.dot_general(qm, kw, (((1,), (1,)), ((), ())), preferred_element_type=F32)
                t = s + bm_ref[h] + rmask
                m = jnp.max(t, axis=-1, keepdims=True)
                pe = jnp.exp(t - m)
                l = jnp.sum(pe, axis=-1, keepdims=True)
                pv = jnp.dot(pe.astype(BF16), vw, preferred_element_type=F32)
                outs.append(pv * (1.0 / l))
                st = jnp.where(lane == h, m + jnp.log(l), st)
            o_ref[0, pl.ds(r0, ATT_SB), ls] = jnp.where(low, outs[0], outs[1]).astype(BF16)
        st_ref[0, pl.ds(r0, ATT_SB), :] = st
        return carry

    lax.fori_loop(0, tq // ATT_SB, sub_block, 0)


def _attn_bias(slopes, dilation):
    qi = jnp.arange(ATT_SB)[:, None]
    ci = jnp.arange(ATT_KW)[None, :]
    delta = ci - A_HALF - qi
    dist = (jnp.abs(delta) * dilation).astype(F32)
    bias = -slopes[:, None, None] * dist[None]
    return jnp.where((jnp.abs(delta) <= A_HALF)[None], bias, NEG)


def _attention_group(proj, bsz, seq, g, dilation, bm):
    sub = seq // dilation
    tq = min(ATT_TQ, sub)
    nq = sub // tq
    hb = tq // A_HALF
    nhb = sub // A_HALF
    pw = PROJ_W // A_WIDTH
    view = proj.reshape(bsz, sub, dilation * PROJ_W)
    cq = COL_QA // A_WIDTH + g
    ck = COL_KA // A_WIDTH + g
    cv = COL_VA // A_WIDTH + g

    def own(c):
        return pl.BlockSpec((1, tq, A_WIDTH), lambda b, r, i: (b, i, r * pw + c))

    def prev(c):
        return pl.BlockSpec((1, A_HALF, A_WIDTH),
                            lambda b, r, i: (b, jnp.maximum(i * hb - 1, 0), r * pw + c))

    def nxt(c):
        return pl.BlockSpec((1, A_HALF, A_WIDTH),
                            lambda b, r, i: (b, jnp.minimum((i + 1) * hb, nhb - 1), r * pw + c))

    o, st = pl.pallas_call(
        functools.partial(_attn_kernel, tq=tq, sub=sub),
        grid=(bsz, dilation, nq),
        in_specs=[own(cq), prev(ck), own(ck), nxt(ck), prev(cv), own(cv), nxt(cv),
                  _resident((A_HEADS, ATT_SB, ATT_KW))],
        out_specs=[pl.BlockSpec((1, tq, A_WIDTH), lambda b, r, i: (b, i, r)),
                   pl.BlockSpec((1, tq, 128), lambda b, r, i: (b, i, r))],
        out_shape=[jax.ShapeDtypeStruct((bsz, sub, dilation * A_WIDTH), BF16),
                   jax.ShapeDtypeStruct((bsz, sub, dilation * 128), F32)],
        scratch_shapes=[pltpu.VMEM((tq + 2 * A_HALF, A_WIDTH), BF16),
                        pltpu.VMEM((tq + 2 * A_HALF, A_WIDTH), BF16)],
        compiler_params=_cparams("parallel", "parallel", "arbitrary"),
        name=f"attn_d{dilation}",
    )(view, view, view, view, view, view, view, bm)
    return o.reshape(bsz * seq, A_WIDTH), st.reshape(bsz * seq, 128)


def _cumsum_rows(x):
    row = lax.broadcasted_iota(jnp.int32, x.shape, 0)
    s = 1
    while s < x.shape[0]:
        x = x + jnp.where(row >= s, pltpu.roll(x, s, axis=0), 0.0)
        s *= 2
    return x


def _gla_kernel(q_ref, k_ref, v_ref, r_ref, gl_ref, wgf_ref, wgb_ref, bf_ref, bb_ref, gn_ref,
                o_ref, oacc, sf, sb, *, seq):
    c = B_CHUNK
    n_chunks = seq // c
    sf[...] = jnp.zeros_like(sf)
    sb[...] = jnp.zeros_like(sb)
    ri = lax.broadcasted_iota(jnp.int32, (c, c), 0)
    ci = lax.broadcasted_iota(jnp.int32, (c, c), 1)
    lane = lax.broadcasted_iota(jnp.int32, (2 * c, 128), 1)
    low = lane < c
    qscale = B_DK ** -0.5

    def direction(r0, wg_ref, b_ref, forward):
        q = q_ref[0, pl.ds(r0, c), :].astype(F32) * qscale
        k = k_ref[0, pl.ds(r0, c), :].astype(F32)
        z = jnp.dot(gl_ref[0, pl.ds(r0, c), :], wg_ref[0], preferred_element_type=F32) + b_ref[0]
        la = (jnp.minimum(z, 0.0) - jnp.log1p(jnp.exp(-jnp.abs(z)))) / B_GATE_TAU
        cs = _cumsum_rows(la)
        tot = cs[c - 1:c, :]
        if forward:
            bq = cs
            ekd = jnp.exp(tot - cs)
            mask = ci <= ri
        else:
            bq = tot - cs + la
            ekd = jnp.exp(cs - la)
            mask = ci > ri
        qe = (q * jnp.exp(bq)).astype(BF16)
        ke = (k * jnp.exp(-bq)).astype(BF16)
        kd = k * ekd
        att = lax.dot_general(qe, ke, (((1,), (1,)), ((), ())), preferred_element_type=F32)
        att = jnp.where(mask, att, 0.0).astype(BF16)
        return qe, att, kd, jnp.exp(tot)

    def step(n):
        rf = pl.multiple_of(n * c, c)
        rb = pl.multiple_of((n_chunks - 1 - n) * c, c)
        qe_f, att_f, kd_f, dec_f = direction(rf, wgf_ref, bf_ref, True)
        qe_b, att_b, kd_b, dec_b = direction(rb, wgb_ref, bb_ref, False)
        v_f = v_ref[0, pl.ds(rf, c), :]
        v_b = v_ref[0, pl.ds(rb, c), :]
        s_f = sf[...]
        s_b = sb[...]
        o_f = (jnp.dot(att_f, v_f, preferred_element_type=F32)
               + jnp.dot(qe_f, s_f.astype(BF16), preferred_element_type=F32))
        o_b = (jnp.dot(att_b, v_b, preferred_element_type=F32)
               + jnp.dot(qe_b, s_b.astype(BF16), preferred_element_type=F32))
        kdt = jnp.concatenate([kd_f, kd_b], axis=0).T
        vst = jnp.concatenate([v_f, v_b], axis=0)
        up_f = jnp.dot(jnp.where(low, kdt, 0.0).astype(BF16), vst, preferred_element_type=F32)
        up_b = jnp.dot(jnp.where(low, 0.0, kdt).astype(BF16), vst, preferred_element_type=F32)
        dcol_f = jnp.broadcast_to(dec_f, (B_DK, B_DK)).T
        dcol_b = jnp.broadcast_to(dec_b, (B_DK, B_DK)).T
        sf[...] = s_f * jnp.concatenate([dcol_f, dcol_f], axis=1) + up_f
        sb[...] = s_b * jnp.concatenate([dcol_b, dcol_b], axis=1) + up_b
        return rf, rb, o_f, o_b

    def first_half(n, carry):
        rf, rb, o_f, o_b = step(n)
        oacc[pl.ds(rf, c), :] = o_f
        oacc[pl.ds(rb, c), :] = o_b
        return carry

    def finish(r0, o):
        y = _rms(o, gn_ref[...])
        rr = r_ref[0, pl.ds(r0, c), :].astype(F32)
        o_ref[0, pl.ds(r0, c), :] = (y * (rr * _sigmoid(rr))).astype(BF16)

    def second_half(n, carry):
        rf, rb, o_f, o_b = step(n)
        finish(rf, o_f + oacc[pl.ds(rf, c), :])
        finish(rb, o_b + oacc[pl.ds(rb, c), :])
        return carry

    lax.fori_loop(0, n_chunks // 2, first_half, 0)
    lax.fori_loop(n_chunks // 2, n_chunks, second_half, 0)


def _gla(proj, bsz, seq, wgf, wgb, bgf, bgb, gn):
    view = proj.reshape(bsz, seq, PROJ_W)

    def cols(width, col0):
        return pl.BlockSpec((1, seq, width), lambda b, h: (b, 0, col0 // width + h))

    def per_head(shape):
        return pl.BlockSpec((1,) + shape, lambda b, h: (h, 0, 0))

    out = pl.pallas_call(
        functools.partial(_gla_kernel, seq=seq),
        grid=(bsz, B_HEADS),
        in_specs=[cols(B_DK, COL_QB), cols(B_DK, COL_KB), cols(B_DV, COL_VB), cols(B_DV, COL_RB),
                  pl.BlockSpec((1, seq, 128), lambda b, h: (b, 0, COL_GL // 128)),
                  per_head((128, B_DK)), per_head((128, B_DK)),
                  per_head((1, B_DK)), per_head((1, B_DK)),
                  pl.BlockSpec((1, B_DV), lambda b, h: (0, 0))],
        out_specs=pl.BlockSpec((1, seq, B_DV), lambda b, h: (b, 0, h)),
        out_shape=jax.ShapeDtypeStruct((bsz, seq, B_VW), BF16),
        scratch_shapes=[pltpu.VMEM((seq, B_DV), F32),
                        pltpu.VMEM((B_DK, B_DV), F32),
                        pltpu.VMEM((B_DK, B_DV), F32)],
        compiler_params=_cparams("parallel", "arbitrary"),
        name="gla",
    )(view, view, view, view, view, wgf, wgb, bgf, bgb, gn)
    return out.reshape(bsz * seq, B_VW)


def _merge_kernel(o1_ref, o2_ref, o3_ref, s1_ref, s2_ref, s3_ref, ob_ref, ga_ref, gb_ref, x_ref,
                  e_ref, wa_ref, wb_ref, wo_ref, out_ref):
    lses = (s1_ref[...], s2_ref[...], s3_ref[...])
    m = jnp.maximum(jnp.maximum(lses[0], lses[1]), lses[2])
    es = [jnp.exp(s - m) for s in lses]
    inv = 1.0 / (es[0] + es[1] + es[2])
    oa = jnp.zeros((ROW_TM, A_WIDTH), F32)
    for o_ref, e in zip((o1_ref, o2_ref, o3_ref), es):
        w = jnp.dot((e * inv).astype(BF16), e_ref[...], preferred_element_type=F32)
        oa = oa + o_ref[...].astype(F32) * w
    ya = jnp.dot(oa.astype(BF16), wa_ref[...], preferred_element_type=F32)
    yb = jnp.dot(ob_ref[...], wb_ref[...], preferred_element_type=F32)
    merged = _sigmoid(ga_ref[...].astype(F32)) * ya + _sigmoid(gb_ref[...].astype(F32)) * yb
    out_ref[...] = x_ref[...] + jnp.dot(merged.astype(BF16), wo_ref[...], preferred_element_type=F32)


def _merge(oas, sts, ob, proj, x, expand, wa, wb, wo):
    t = x.shape[0]

    def rows(width, cblk=0):
        return pl.BlockSpec((ROW_TM, width), lambda i: (i, cblk))

    return pl.pallas_call(
        _merge_kernel,
        grid=(t // ROW_TM,),
        in_specs=[rows(A_WIDTH)] * 3 + [rows(128)] * 3
                 + [rows(B_VW), rows(D_MODEL, COL_GA // D_MODEL), rows(D_MODEL, COL_GB // D_MODEL),
                    rows(D_MODEL),
                    _resident((128, A_WIDTH)), _resident((A_WIDTH, D_MODEL)),
                    _resident((B_VW, D_MODEL)), _resident((D_MODEL, D_MODEL))],
        out_specs=rows(D_MODEL),
        out_shape=jax.ShapeDtypeStruct((t, D_MODEL), F32),
        compiler_params=_cparams("parallel"),
        name="merge",
    )(*oas, *sts, ob, proj, proj, x, expand, wa, wb, wo)


def _ffn_kernel(x_ref, g_ref, w1_ref, w2_ref, fg_ref, out_ref, *, final):
    x = x_ref[...]
    h = _rms(x, g_ref[...]).astype(BF16)
    acc = x
    for c in range(D_FF // D_MODEL):
        sl = slice(c * D_MODEL, (c + 1) * D_MODEL)
        u = jnp.maximum(jnp.dot(h, w1_ref[:, sl], preferred_element_type=F32), 0.0)
        acc = acc + jnp.dot((u * u).astype(BF16), w2_ref[sl, :], preferred_element_type=F32)
    if final:
        acc = _rms(acc, fg_ref[...])
    out_ref[...] = acc


def _ffn(x, g, w1, w2, fg, final):
    t = x.shape[0]
    return pl.pallas_call(
        functools.partial(_ffn_kernel, final=final),
        grid=(t // ROW_TM,),
        in_specs=[pl.BlockSpec((ROW_TM, D_MODEL), lambda i: (i, 0)),
                  _resident((1, D_MODEL)), _resident((D_MODEL, D_FF)), _resident((D_FF, D_MODEL)),
                  _resident((1, D_MODEL))],
        out_specs=pl.BlockSpec((ROW_TM, D_MODEL), lambda i: (i, 0)),
        out_shape=jax.ShapeDtypeStruct((t, D_MODEL), F32),
        compiler_params=_cparams("parallel"),
        name="ffn_final" if final else "ffn",
    )(x, g, w1, w2, fg)


def _prep_layer(l, norm_mix_g, w_in, wgf, bgf, wgb, bgb, gla_norm_g, w_branch_a, w_branch_b, w_out,
                norm_ffn_g, w_ff1, w_ff2):
    w = w_in[l]
    gl0 = 3 * A_QKV_W + 2 * B_KW + 2 * B_VW
    gl1 = gl0 + 2 * B_GATE_RANK
    w_fused = jnp.concatenate(
        [w[:, :gl1], jnp.zeros((D_MODEL, GL_PAD - 2 * B_GATE_RANK), w.dtype), w[:, gl1:]], axis=1)

    def gate_map(wg, row0):
        per_head = wg.reshape(B_GATE_RANK, B_HEADS, B_DK).transpose(1, 0, 2)
        return jnp.pad(per_head, ((0, 0), (row0, 128 - row0 - B_GATE_RANK), (0, 0))).astype(BF16)

    return dict(
        g_mix=norm_mix_g[l].reshape(1, D_MODEL),
        w_in=w_fused.astype(BF16),
        wgf=gate_map(wgf[l], 0),
        wgb=gate_map(wgb[l], B_GATE_RANK),
        bgf=bgf[l].reshape(B_HEADS, 1, B_DK),
        bgb=bgb[l].reshape(B_HEADS, 1, B_DK),
        gn=gla_norm_g[l].reshape(1, B_DV),
        wa=w_branch_a[l].astype(BF16),
        wb=w_branch_b[l].astype(BF16),
        wo=w_out[l].astype(BF16),
        g_ffn=norm_ffn_g[l].reshape(1, D_MODEL),
        w1=w_ff1[l].astype(BF16),
        w2=w_ff2[l].astype(BF16),
    )


def _trunk(x, layers, biases, expand, final_g):
    bsz, seq, _ = x.shape
    xf = x.reshape(bsz * seq, D_MODEL)
    for li, p in enumerate(layers):
        proj = _inproj(xf, p["g_mix"], p["w_in"])
        oas, sts = [], []
        for g, (_, dilation) in enumerate(A_GROUPS):
            o, st = _attention_group(proj, bsz, seq, g, dilation, biases[g])
            oas.append(o)
            sts.append(st)
        ob = _gla(proj, bsz, seq, p["wgf"], p["wgb"], p["bgf"], p["bgb"], p["gn"])
        x1 = _merge(oas, sts, ob, proj, xf, expand, p["wa"], p["wb"], p["wo"])
        xf = _ffn(x1, p["g_ffn"], p["w1"], p["w2"], final_g, final=(li == len(layers) - 1))
    return xf.reshape(bsz, seq, D_MODEL)


def kernel(x_prompt, x_sample, norm_mix_g, w_in, gla_w_gate_fwd, gla_b_gate_fwd, gla_w_gate_bwd,
           gla_b_gate_bwd, gla_norm_g, w_branch_a, w_branch_b, w_out, norm_ffn_g, w_ff1, w_ff2,
           final_norm_g):
    layers = [_prep_layer(l, norm_mix_g, w_in, gla_w_gate_fwd, gla_b_gate_fwd, gla_w_gate_bwd,
                          gla_b_gate_bwd, gla_norm_g, w_branch_a, w_branch_b, w_out, norm_ffn_g,
                          w_ff1, w_ff2) for l in range(DEPTH)]
    n_heads = len(A_GROUPS) * A_HEADS
    slopes = jnp.exp2(-8.0 * jnp.arange(1, n_heads + 1, dtype=F32) / n_heads)
    biases = [_attn_bias(slopes[g * A_HEADS:(g + 1) * A_HEADS], dilation)
              for g, (_, dilation) in enumerate(A_GROUPS)]
    expand = (jnp.arange(128)[:, None] == jnp.arange(A_WIDTH)[None, :] // A_HEAD_DIM).astype(BF16)
    final_g = final_norm_g.reshape(1, D_MODEL)
    return (_trunk(x_prompt, layers, biases, expand, final_g),
            _trunk(x_sample, layers, biases, expand, final_g))
```

```python
import functools

import jax
import jax.numpy as jnp
from jax import lax
from jax.experimental import pallas as pl
from jax.experimental.pallas import tpu as pltpu

F32 = jnp.float32
BF16 = jnp.bfloat16

D_MODEL = 1024
DEPTH = 2
A_GROUPS = ((128, 1), (512, 4), (2048, 16))
A_HEADS = 8
A_HEAD_DIM = 64
A_WIDTH = A_HEADS * A_HEAD_DIM
A_QKV_W = 3 * A_WIDTH
A_HALF = 64
B_HEADS = 4
B_DK = 128
B_DV = 256
B_KW = B_HEADS * B_DK
B_VW = B_HEADS * B_DV
B_GATE_RANK = 16
B_GATE_TAU = 16.0
B_CHUNK = 64
D_FF = 4 * D_MODEL
EPS = 1e-6
NEG = -1e30
LANES = 128

COL_A = 0
COL_QB = COL_A + len(A_GROUPS) * A_QKV_W
COL_KB = COL_QB + B_KW
COL_VB = COL_KB + B_KW
COL_RB = COL_VB + B_VW
COL_GL = COL_RB + B_VW
GL_PAD = 512
COL_GA = COL_GL + GL_PAD
COL_GB = COL_GA + D_MODEL
PROJ_W = COL_GB + D_MODEL

VMEM_LIMIT_BYTES = 56 * 1024 * 1024

PERM_ROWS = 1024
IN_TM = PERM_ROWS
IN_TN = 2048
IN_SUB = 512
ROW_TM = 512
ATT_TQ = 512
ATT_SB = 128
ATT_KW = ATT_SB + 2 * A_HALF


def _cparams(*sem):
    return pltpu.CompilerParams(dimension_semantics=sem, vmem_limit_bytes=VMEM_LIMIT_BYTES)


def _resident(shape):
    nd = len(shape)
    return pl.BlockSpec(shape, lambda *_: (0,) * nd, pipeline_mode=pl.Buffered(1))


def _rms(x, g):
    return x * lax.rsqrt(jnp.mean(x * x, axis=-1, keepdims=True) + EPS) * g


def _sigmoid(t):
    return 1.0 / (1.0 + jnp.exp(-t))


def _inproj_kernel(x_ref, g_ref, w_ref, o_ref, hf_ref, hs_ref):
    j = pl.program_id(1)

    @pl.when(j == 0)
    def _():
        h = _rms(x_ref[...], g_ref[...])
        hs_ref[0] = h.astype(BF16)
        for s in range(D_MODEL // LANES):
            hf_ref[s] = h[:, s * LANES:(s + 1) * LANES]
        for k, (_, d) in enumerate(A_GROUPS):
            if d == 1:
                continue
            n = IN_TM // d
            for r in range(d):
                for s in range(D_MODEL // LANES):
                    hs_ref[k, r * n:(r + 1) * n, s * LANES:(s + 1) * LANES] = (
                        hf_ref[s, pl.ds(r, n, stride=d), :].astype(BF16))

    chunks_per_group = A_QKV_W // IN_SUB
    for c in range(IN_TN // IN_SUB):
        cc = j * (IN_TN // IN_SUB) + c
        k = jnp.where(cc < len(A_GROUPS) * chunks_per_group, cc // chunks_per_group, 0)
        sl = slice(c * IN_SUB, (c + 1) * IN_SUB)
        o_ref[:, sl] = jnp.dot(hs_ref[k], w_ref[:, sl], preferred_element_type=F32).astype(BF16)


def _inproj(x, g, w):
    t = x.shape[0]
    return pl.pallas_call(
        _inproj_kernel,
        grid=(t // IN_TM, PROJ_W // IN_TN),
        in_specs=[
            pl.BlockSpec((IN_TM, D_MODEL), lambda i, j: (i, 0)),
            pl.BlockSpec((1, D_MODEL), lambda i, j: (0, 0)),
            pl.BlockSpec((D_MODEL, IN_TN), lambda i, j: (0, j)),
        ],
        out_specs=pl.BlockSpec((IN_TM, IN_TN), lambda i, j: (i, j)),
        out_shape=jax.ShapeDtypeStruct((t, PROJ_W), BF16),
        scratch_shapes=[pltpu.VMEM((D_MODEL // LANES, IN_TM, LANES), F32),
                        pltpu.VMEM((len(A_GROUPS), IN_TM, D_MODEL), BF16)],
        compiler_params=_cparams("parallel", "arbitrary"),
        name="inproj",
    )(x, g, w)


def _attn_kernel(q_ref, kp_ref, k_ref, kn_ref, vp_ref, v_ref, vn_ref, bm_ref, o_ref, st_ref,
                 qbuf, kext, vext, obuf, stbuf, *, tq, sub, n_runs, run):
    i = pl.program_id(2)

    def gather(dst, off, src):
        if n_runs is None:
            dst[off:off + tq] = src[...]
        else:
            for c in range(n_runs):
                dst[off + c * run:off + (c + 1) * run] = src[c]

    gather(qbuf, 0, q_ref)
    kext[0:A_HALF] = kp_ref[...]
    gather(kext, A_HALF, k_ref)
    kext[A_HALF + tq:] = kn_ref[...]
    vext[0:A_HALF] = vp_ref[...]
    gather(vext, A_HALF, v_ref)
    vext[A_HALF + tq:] = vn_ref[...]

    lane = lax.broadcasted_iota(jnp.int32, (ATT_SB, LANES), 1)
    low = lane < A_HEAD_DIM
    col = lax.broadcasted_iota(jnp.int32, (ATT_SB, ATT_KW), 1)
    scale = A_HEAD_DIM ** -0.5

    def sub_block(jq, carry):
        r0 = pl.multiple_of(jq * ATT_SB, ATT_SB)
        kidx = i * tq + jq * ATT_SB - A_HALF + col
        rmask = jnp.where((kidx >= 0) & (kidx < sub), 0.0, NEG)
        st = jnp.zeros((ATT_SB, LANES), F32)
        for p in range(A_HEADS // 2):
            ls = slice(p * LANES, (p + 1) * LANES)
            qp = qbuf[pl.ds(r0, ATT_SB), ls] * scale
            kw = kext[pl.ds(r0, ATT_KW), ls]
            vw = vext[pl.ds(r0, ATT_KW), ls]
            outs = []
            for hh in range(2):
                h = 2 * p + hh
                qm = jnp.where(low if hh == 0 else jnp.logical_not(low), qp, jnp.zeros_like(qp))
                s = lax.dot_general(qm, kw, (((1,), (1,)), ((), ())), preferred_element_type=F32)
                t = s + bm_ref[h] + rmask
                m = jnp.max(t, axis=-1, keepdims=True)
                pe = jnp.exp(t - m)
                l = jnp.sum(pe, axis=-1, keepdims=True)
                pv = jnp.dot(pe.astype(BF16), vw, preferred_element_type=F32)
                outs.append(pv * (1.0 / l))
                st = jnp.where(lane == h, m + jnp.log(l), st)
            obuf[pl.ds(r0, ATT_SB), ls] = jnp.where(low, outs[0], outs[1]).astype(BF16)
        stbuf[pl.ds(r0, ATT_SB), :] = st
        return carry

    lax.fori_loop(0, tq // ATT_SB, sub_block, 0)

    if n_runs is None:
        o_ref[...] = obuf[...]
        st_ref[...] = stbuf[...]
    else:
        for c in range(n_runs):
            o_ref[c] = obuf[c * run:(c + 1) * run]
            st_ref[c] = stbuf[c * run:(c + 1) * run]


def _attn_bias(slopes, dilation):
    qi = jnp.arange(ATT_SB)[:, None]
    ci = jnp.arange(ATT_KW)[None, :]
    delta = ci - A_HALF - qi
    dist = (jnp.abs(delta) * dilation).astype(F32)
    bias = -slopes[:, None, None] * dist[None]
    return jnp.where((jnp.abs(delta) <= A_HALF)[None], bias, NEG)


def _attention_group(proj, bsz, seq, g, dilation, bm):
    sub = seq // dilation
    run = PERM_ROWS // dilation
    tq = min(ATT_TQ, sub)
    nq = sub // tq
    n_tiles = seq // PERM_ROWS
    view = proj.reshape(bsz, n_tiles, dilation, run, PROJ_W)
    cq = (COL_A + g * A_QKV_W) // A_WIDTH
    ck = cq + 1
    cv = cq + 2
    halo_per_run = run // A_HALF
    n_halo = sub // A_HALF
    halo_per_tq = tq // A_HALF

    if tq >= run:
        n_runs = tq // run

        def main(width, c):
            return pl.BlockSpec((None, n_runs, None, run, width), lambda b, r, i: (b, i, r, 0, c))
    else:
        n_runs = None
        per_run = run // tq

        def main(width, c):
            return pl.BlockSpec((None, None, None, tq, width),
                                lambda b, r, i: (b, i // per_run, r, i % per_run, c))

    def halo(c, which):
        def index(b, r, i):
            if which == "prev":
                hb = jnp.maximum(i * halo_per_tq - 1, 0)
            else:
                hb = jnp.minimum((i + 1) * halo_per_tq, n_halo - 1)
            return (b, hb // halo_per_run, r, hb % halo_per_run, c)
        return pl.BlockSpec((None, None, None, A_HALF, A_WIDTH), index)

    o, st = pl.pallas_call(
        functools.partial(_attn_kernel, tq=tq, sub=sub, n_runs=n_runs, run=run),
        grid=(bsz, dilation, nq),
        in_specs=[main(A_WIDTH, cq), halo(ck, "prev"), main(A_WIDTH, ck), halo(ck, "next"),
                  halo(cv, "prev"), main(A_WIDTH, cv), halo(cv, "next"),
                  _resident((A_HEADS, ATT_SB, ATT_KW))],
        out_specs=[main(A_WIDTH, 0), main(LANES, 0)],
        out_shape=[jax.ShapeDtypeStruct((bsz, n_tiles, dilation, run, A_WIDTH), BF16),
                   jax.ShapeDtypeStruct((bsz, n_tiles, dilation, run, LANES), F32)],
        scratch_shapes=[pltpu.VMEM((tq, A_WIDTH), BF16),
                        pltpu.VMEM((tq + 2 * A_HALF, A_WIDTH), BF16),
                        pltpu.VMEM((tq + 2 * A_HALF, A_WIDTH), BF16),
                        pltpu.VMEM((tq, A_WIDTH), BF16),
                        pltpu.VMEM((tq, LANES), F32)],
        compiler_params=_cparams("parallel", "parallel", "arbitrary"),
        name=f"attn_d{dilation}",
    )(view, view, view, view, view, view, view, bm)
    n_perm = bsz * n_tiles
    return (o.reshape(n_perm, dilation, run, A_WIDTH), st.reshape(n_perm, dilation, run, LANES))


def _cumsum_rows(x):
    row = lax.broadcasted_iota(jnp.int32, x.shape, 0)
    s = 1
    while s < x.shape[0]:
        x = x + jnp.where(row >= s, pltpu.roll(x, s, axis=0), 0.0)
        s *= 2
    return x


def _gla_kernel(q_ref, k_ref, v_ref, r_ref, gl_ref, wgf_ref, wgb_ref, bf_ref, bb_ref, gn_ref,
                o_ref, oacc, sf, sb, *, seq):
    c = B_CHUNK
    n_chunks = seq // c
    sf[...] = jnp.zeros_like(sf)
    sb[...] = jnp.zeros_like(sb)
    ri = lax.broadcasted_iota(jnp.int32, (c, c), 0)
    ci = lax.broadcasted_iota(jnp.int32, (c, c), 1)
    lane = lax.broadcasted_iota(jnp.int32, (2 * c, LANES), 1)
    low = lane < c
    qscale = B_DK ** -0.5

    def direction(r0, wg_ref, b_ref, forward):
        q = q_ref[0, pl.ds(r0, c), :].astype(F32) * qscale
        k = k_ref[0, pl.ds(r0, c), :].astype(F32)
        z = jnp.dot(gl_ref[0, pl.ds(r0, c), :], wg_ref[0], preferred_element_type=F32) + b_ref[0]
        la = (jnp.minimum(z, 0.0) - jnp.log1p(jnp.exp(-jnp.abs(z)))) / B_GATE_TAU
        cs = _cumsum_rows(la)
        tot = cs[c - 1:c, :]
        if forward:
            bq = cs
            ekd = jnp.exp(tot - cs)
            mask = ci <= ri
        else:
            bq = tot - cs + la
            ekd = jnp.exp(cs - la)
            mask = ci > ri
        qe = (q * jnp.exp(bq)).astype(BF16)
        ke = (k * jnp.exp(-bq)).astype(BF16)
        kd = k * ekd
        att = lax.dot_general(qe, ke, (((1,), (1,)), ((), ())), preferred_element_type=F32)
        att = jnp.where(mask, att, 0.0).astype(BF16)
        return qe, att, kd, jnp.exp(tot)

    def step(n):
        rf = pl.multiple_of(n * c, c)
        rb = pl.multiple_of((n_chunks - 1 - n) * c, c)
        qe_f, att_f, kd_f, dec_f = direction(rf, wgf_ref, bf_ref, True)
        qe_b, att_b, kd_b, dec_b = direction(rb, wgb_ref, bb_ref, False)
        v_f = v_ref[0, pl.ds(rf, c), :]
        v_b = v_ref[0, pl.ds(rb, c), :]
        s_f = sf[...]
        s_b = sb[...]
        o_f = (jnp.dot(att_f, v_f, preferred_element_type=F32)
               + jnp.dot(qe_f, s_f.astype(BF16), preferred_element_type=F32))
        o_b = (jnp.dot(att_b, v_b, preferred_element_type=F32)
               + jnp.dot(qe_b, s_b.astype(BF16), preferred_element_type=F32))
        kdt = jnp.concatenate([kd_f, kd_b], axis=0).T
        vst = jnp.concatenate([v_f, v_b], axis=0)
        up_f = jnp.dot(jnp.where(low, kdt, 0.0).astype(BF16), vst, preferred_element_type=F32)
        up_b = jnp.dot(jnp.where(low, 0.0, kdt).astype(BF16), vst, preferred_element_type=F32)
        dcol_f = jnp.broadcast_to(dec_f, (B_DK, B_DK)).T
        dcol_b = jnp.broadcast_to(dec_b, (B_DK, B_DK)).T
        sf[...] = s_f * jnp.concatenate([dcol_f, dcol_f], axis=1) + up_f
        sb[...] = s_b * jnp.concatenate([dcol_b, dcol_b], axis=1) + up_b
        return rf, rb, o_f, o_b

    def first_half(n, carry):
        rf, rb, o_f, o_b = step(n)
        oacc[pl.ds(rf, c), :] = o_f
        oacc[pl.ds(rb, c), :] = o_b
        return carry

    def finish(r0, o):
        y = _rms(o, gn_ref[...])
        rr = r_ref[0, pl.ds(r0, c), :].astype(F32)
        o_ref[0, pl.ds(r0, c), :] = (y * (rr * _sigmoid(rr))).astype(BF16)

    def second_half(n, carry):
        rf, rb, o_f, o_b = step(n)
        finish(rf, o_f + oacc[pl.ds(rf, c), :])
        finish(rb, o_b + oacc[pl.ds(rb, c), :])
        return carry

    lax.fori_loop(0, n_chunks // 2, first_half, 0)
    lax.fori_loop(n_chunks // 2, n_chunks, second_half, 0)


def _gla(proj, bsz, seq, wgf, wgb, bgf, bgb, gn):
    view = proj.reshape(bsz, seq, PROJ_W)

    def cols(width, col0):
        return pl.BlockSpec((1, seq, width), lambda b, h: (b, 0, col0 // width + h))

    def per_head(shape):
        return pl.BlockSpec((1,) + shape, lambda b, h: (h, 0, 0))

    out = pl.pallas_call(
        functools.partial(_gla_kernel, seq=seq),
        grid=(bsz, B_HEADS),
        in_specs=[cols(B_DK, COL_QB), cols(B_DK, COL_KB), cols(B_DV, COL_VB), cols(B_DV, COL_RB),
                  pl.BlockSpec((1, seq, LANES), lambda b, h: (b, 0, COL_GL // LANES)),
                  per_head((LANES, B_DK)), per_head((LANES, B_DK)),
                  per_head((1, B_DK)), per_head((1, B_DK)),
                  pl.BlockSpec((1, B_DV), lambda b, h: (0, 0))],
        out_specs=pl.BlockSpec((1, seq, B_DV), lambda b, h: (b, 0, h)),
        out_shape=jax.ShapeDtypeStruct((bsz, seq, B_VW), BF16),
        scratch_shapes=[pltpu.VMEM((seq, B_DV), F32),
                        pltpu.VMEM((B_DK, B_DV), F32),
                        pltpu.VMEM((B_DK, B_DV), F32)],
        compiler_params=_cparams("parallel", "arbitrary"),
        name="gla",
    )(view, view, view, view, view, wgf, wgb, bgf, bgb, gn)
    return out.reshape(bsz * seq, B_VW)


def _merge_kernel(o1_ref, o2_ref, o3_ref, s1_ref, s2_ref, s3_ref, ob_ref, ga_ref, gb_ref, x_ref,
                  e_ref, wa_ref, wb_ref, wo_ref, out_ref, on2, on3, sn2, sn3):
    n_slabs = A_WIDTH // LANES
    for (_, d), o_ref, s_ref, on, sn in ((A_GROUPS[1], o2_ref, s2_ref, on2, sn2),
                                        (A_GROUPS[2], o3_ref, s3_ref, on3, sn3)):
        n = ROW_TM // d
        for r in range(d):
            sn[pl.ds(r, n, stride=d), :] = s_ref[r]
            for s in range(n_slabs):
                on[s, pl.ds(r, n, stride=d), :] = o_ref[r, :, s * LANES:(s + 1) * LANES].astype(F32)

    lses = (s1_ref[...], sn2[...], sn3[...])
    m = jnp.maximum(jnp.maximum(lses[0], lses[1]), lses[2])
    es = [jnp.exp(s - m) for s in lses]
    inv = 1.0 / (es[0] + es[1] + es[2])
    ws = [jnp.dot((e * inv).astype(BF16), e_ref[...], preferred_element_type=F32) for e in es]
    slabs = []
    for s in range(n_slabs):
        ls = slice(s * LANES, (s + 1) * LANES)
        slabs.append(o1_ref[:, ls].astype(F32) * ws[0][:, ls]
                     + on2[s] * ws[1][:, ls] + on3[s] * ws[2][:, ls])
    oa = jnp.concatenate(slabs, axis=1)
    ya = jnp.dot(oa.astype(BF16), wa_ref[...], preferred_element_type=F32)
    yb = jnp.dot(ob_ref[...], wb_ref[...], preferred_element_type=F32)
    merged = _sigmoid(ga_ref[...].astype(F32)) * ya + _sigmoid(gb_ref[...].astype(F32)) * yb
    out_ref[...] = x_ref[...] + jnp.dot(merged.astype(BF16), wo_ref[...], preferred_element_type=F32)


def _merge(oas, sts, ob, proj, x, expand, wa, wb, wo):
    t = x.shape[0]
    per_perm = PERM_ROWS // ROW_TM

    def rows(width, cblk=0):
        return pl.BlockSpec((ROW_TM, width), lambda i: (i, cblk))

    def perm(width, d):
        return pl.BlockSpec((None, d, ROW_TM // d, width), lambda i: (i // per_perm, 0, i % per_perm, 0))

    d2, d3 = A_GROUPS[1][1], A_GROUPS[2][1]
    return pl.pallas_call(
        _merge_kernel,
        grid=(t // ROW_TM,),
        in_specs=[rows(A_WIDTH), perm(A_WIDTH, d2), perm(A_WIDTH, d3),
                  rows(LANES), perm(LANES, d2), perm(LANES, d3),
                  rows(B_VW), rows(D_MODEL, COL_GA // D_MODEL), rows(D_MODEL, COL_GB // D_MODEL),
                  rows(D_MODEL),
                  _resident((LANES, A_WIDTH)), _resident((A_WIDTH, D_MODEL)),
                  _resident((B_VW, D_MODEL)), _resident((D_MODEL, D_MODEL))],
        out_specs=rows(D_MODEL),
        out_shape=jax.ShapeDtypeStruct((t, D_MODEL), F32),
        scratch_shapes=[pltpu.VMEM((A_WIDTH // LANES, ROW_TM, LANES), F32),
                        pltpu.VMEM((A_WIDTH // LANES, ROW_TM, LANES), F32),
                        pltpu.VMEM((ROW_TM, LANES), F32),
                        pltpu.VMEM((ROW_TM, LANES), F32)],
        compiler_params=_cparams("parallel"),
        name="merge",
    )(oas[0].reshape(t, A_WIDTH), oas[1], oas[2], sts[0].reshape(t, LANES), sts[1], sts[2],
      ob, proj, proj, x, expand, wa, wb, wo)


def _ffn_kernel(x_ref, g_ref, w1_ref, w2_ref, fg_ref, out_ref, *, final):
    x = x_ref[...]
    h = _rms(x, g_ref[...]).astype(BF16)
    acc = x
    for c in range(D_FF // D_MODEL):
        sl = slice(c * D_MODEL, (c + 1) * D_MODEL)
        u = jnp.maximum(jnp.dot(h, w1_ref[:, sl], preferred_element_type=F32), 0.0)
        acc = acc + jnp.dot((u * u).astype(BF16), w2_ref[sl, :], preferred_element_type=F32)
    if final:
        acc = _rms(acc, fg_ref[...])
    out_ref[...] = acc


def _ffn(x, g, w1, w2, fg, final):
    t = x.shape[0]
    return pl.pallas_call(
        functools.partial(_ffn_kernel, final=final),
        grid=(t // ROW_TM,),
        in_specs=[pl.BlockSpec((ROW_TM, D_MODEL), lambda i: (i, 0)),
                  _resident((1, D_MODEL)), _resident((D_MODEL, D_FF)), _resident((D_FF, D_MODEL)),
                  _resident((1, D_MODEL))],
        out_specs=pl.BlockSpec((ROW_TM, D_MODEL), lambda i: (i, 0)),
        out_shape=jax.ShapeDtypeStruct((t, D_MODEL), F32),
        compiler_params=_cparams("parallel"),
        name="ffn_final" if final else "ffn",
    )(x, g, w1, w2, fg)


def _prep_layer(l, norm_mix_g, w_in, wgf, bgf, wgb, bgb, gla_norm_g, w_branch_a, w_branch_b, w_out,
                norm_ffn_g, w_ff1, w_ff2):
    w = w_in[l]
    n_groups = len(A_GROUPS)
    a_end = 3 * n_groups * A_WIDTH
    wa = w[:, :a_end].reshape(D_MODEL, 3, n_groups, A_WIDTH).transpose(0, 2, 1, 3).reshape(D_MODEL, a_end)
    gl0 = a_end + 2 * B_KW + 2 * B_VW
    gl1 = gl0 + 2 * B_GATE_RANK
    w_fused = jnp.concatenate(
        [wa, w[:, a_end:gl1], jnp.zeros((D_MODEL, GL_PAD - 2 * B_GATE_RANK), w.dtype), w[:, gl1:]],
        axis=1)

    def gate_map(wg, row0):
        per_head = wg.reshape(B_GATE_RANK, B_HEADS, B_DK).transpose(1, 0, 2)
        return jnp.pad(per_head, ((0, 0), (row0, LANES - row0 - B_GATE_RANK), (0, 0))).astype(BF16)

    return dict(
        g_mix=norm_mix_g[l].reshape(1, D_MODEL),
        w_in=w_fused.astype(BF16),
        wgf=gate_map(wgf[l], 0),
        wgb=gate_map(wgb[l], B_GATE_RANK),
        bgf=bgf[l].reshape(B_HEADS, 1, B_DK),
        bgb=bgb[l].reshape(B_HEADS, 1, B_DK),
        gn=gla_norm_g[l].reshape(1, B_DV),
        wa=w_branch_a[l].astype(BF16),
        wb=w_branch_b[l].astype(BF16),
        wo=w_out[l].astype(BF16),
        g_ffn=norm_ffn_g[l].reshape(1, D_MODEL),
        w1=w_ff1[l].astype(BF16),
        w2=w_ff2[l].astype(BF16),
    )


def _trunk(x, layers, biases, expand, final_g):
    bsz, seq, _ = x.shape
    xf = x.reshape(bsz * seq, D_MODEL)
    for li, p in enumerate(layers):
        proj = _inproj(xf, p["g_mix"], p["w_in"])
        oas, sts = [], []
        for g, (_, dilation) in enumerate(A_GROUPS):
            o, st = _attention_group(proj, bsz, seq, g, dilation, biases[g])
            oas.append(o)
            sts.append(st)
        ob = _gla(proj, bsz, seq, p["wgf"], p["wgb"], p["bgf"], p["bgb"], p["gn"])
        x1 = _merge(oas, sts, ob, proj, xf, expand, p["wa"], p["wb"], p["wo"])
        xf = _ffn(x1, p["g_ffn"], p["w1"], p["w2"], final_g, final=(li == len(layers) - 1))
    return xf.reshape(bsz, seq, D_MODEL)


def kernel(x_prompt, x_sample, norm_mix_g, w_in, gla_w_gate_fwd, gla_b_gate_fwd, gla_w_gate_bwd,
           gla_b_gate_bwd, gla_norm_g, w_branch_a, w_branch_b, w_out, norm_ffn_g, w_ff1, w_ff2,
           final_norm_g):
    layers = [_prep_layer(l, norm_mix_g, w_in, gla_w_gate_fwd, gla_b_gate_fwd, gla_w_gate_bwd,
                          gla_b_gate_bwd, gla_norm_g, w_branch_a, w_branch_b, w_out, norm_ffn_g,
                          w_ff1, w_ff2) for l in range(DEPTH)]
    n_heads = len(A_GROUPS) * A_HEADS
    slopes = jnp.exp2(-8.0 * jnp.arange(1, n_heads + 1, dtype=F32) / n_heads)
    biases = [_attn_bias(slopes[g * A_HEADS:(g + 1) * A_HEADS], dilation)
              for g, (_, dilation) in enumerate(A_GROUPS)]
    expand = (jnp.arange(LANES)[:, None] == jnp.arange(A_WIDTH)[None, :] // A_HEAD_DIM).astype(BF16)
    final_g = final_norm_g.reshape(1, D_MODEL)
    return (_trunk(x_prompt, layers, biases, expand, final_g),
            _trunk(x_sample, layers, biases, expand, final_g))
```

```python
import functools

import jax
import jax.numpy as jnp
from jax import lax
from jax.experimental import pallas as pl
from jax.experimental.pallas import tpu as pltpu

F32 = jnp.float32
BF16 = jnp.bfloat16

D_MODEL = 1024
DEPTH = 2
A_GROUPS = ((128, 1), (512, 4), (2048, 16))
A_HEADS = 8
A_HEAD_DIM = 64
A_WIDTH = A_HEADS * A_HEAD_DIM
A_QKV_W = 3 * A_WIDTH
A_HALF = 64
B_HEADS = 4
B_DK = 128
B_DV = 256
B_KW = B_HEADS * B_DK
B_VW = B_HEADS * B_DV
B_GATE_RANK = 16
B_GATE_TAU = 16.0
B_CHUNK = 64
D_FF = 4 * D_MODEL
EPS = 1e-6
NEG = -1e30
LOG2E = 1.4426950408889634
LN2 = 0.6931471805599453
LANES = 128

COL_A = 0
COL_QB = COL_A + len(A_GROUPS) * A_QKV_W
COL_KB = COL_QB + B_KW
COL_VB = COL_KB + B_KW
COL_RB = COL_VB + B_VW
COL_GL = COL_RB + B_VW
GL_PAD = 512
COL_GA = COL_GL + GL_PAD
COL_GB = COL_GA + D_MODEL
PROJ_W = COL_GB + D_MODEL

VMEM_LIMIT_BYTES = 56 * 1024 * 1024

PERM_ROWS = 1024
IN_TM = PERM_ROWS
IN_TN = 2048
IN_SUB = 512
ROW_TM = 512
ATT_TQ = 512
ATT_SB = 128
ATT_KW = ATT_SB + 2 * A_HALF
GLA_UNROLL = 8


def _cparams(*sem):
    return pltpu.CompilerParams(dimension_semantics=sem, vmem_limit_bytes=VMEM_LIMIT_BYTES)


def _resident(shape):
    nd = len(shape)
    return pl.BlockSpec(shape, lambda *_: (0,) * nd, pipeline_mode=pl.Buffered(1))


def _rms(x, g):
    return x * lax.rsqrt(jnp.mean(x * x, axis=-1, keepdims=True) + EPS) * g


def _sigmoid(t):
    return 1.0 / (1.0 + jnp.exp(-t))


def _inproj_kernel(x_ref, g_ref, w_ref, o_ref, hf_ref, hs_ref):
    j = pl.program_id(1)

    @pl.when(j == 0)
    def _():
        h = _rms(x_ref[...], g_ref[...])
        hs_ref[0] = h.astype(BF16)
        for s in range(D_MODEL // LANES):
            hf_ref[s] = h[:, s * LANES:(s + 1) * LANES]
        for k, (_, d) in enumerate(A_GROUPS):
            if d == 1:
                continue
            n = IN_TM // d
            for r in range(d):
                for s in range(D_MODEL // LANES):
                    hs_ref[k, r * n:(r + 1) * n, s * LANES:(s + 1) * LANES] = (
                        hf_ref[s, pl.ds(r, n, stride=d), :].astype(BF16))

    chunks_per_group = A_QKV_W // IN_SUB
    for c in range(IN_TN // IN_SUB):
        cc = j * (IN_TN // IN_SUB) + c
        k = jnp.where(cc < len(A_GROUPS) * chunks_per_group, cc // chunks_per_group, 0)
        sl = slice(c * IN_SUB, (c + 1) * IN_SUB)
        o_ref[:, sl] = jnp.dot(hs_ref[k], w_ref[:, sl], preferred_element_type=F32).astype(BF16)


def _inproj(x, g, w):
    t = x.shape[0]
    return pl.pallas_call(
        _inproj_kernel,
        grid=(t // IN_TM, PROJ_W // IN_TN),
        in_specs=[
            pl.BlockSpec((IN_TM, D_MODEL), lambda i, j: (i, 0)),
            pl.BlockSpec((1, D_MODEL), lambda i, j: (0, 0)),
            pl.BlockSpec((D_MODEL, IN_TN), lambda i, j: (0, j)),
        ],
        out_specs=pl.BlockSpec((IN_TM, IN_TN), lambda i, j: (i, j)),
        out_shape=jax.ShapeDtypeStruct((t, PROJ_W), BF16),
        scratch_shapes=[pltpu.VMEM((D_MODEL // LANES, IN_TM, LANES), F32),
                        pltpu.VMEM((len(A_GROUPS), IN_TM, D_MODEL), BF16)],
        compiler_params=_cparams("parallel", "arbitrary"),
        name="inproj",
    )(x, g, w)


def _attn_kernel(q_ref, kp_ref, k_ref, kn_ref, vp_ref, v_ref, vn_ref, bm_ref, o_ref, st_ref,
                 qbuf, kext, vext, obuf, stbuf, *, tq, sub, n_runs, run):
    i = pl.program_id(2)

    def gather(dst, off, src):
        if n_runs is None:
            dst[off:off + tq] = src[...]
        else:
            for c in range(n_runs):
                dst[off + c * run:off + (c + 1) * run] = src[c]

    gather(qbuf, 0, q_ref)
    kext[0:A_HALF] = kp_ref[...]
    gather(kext, A_HALF, k_ref)
    kext[A_HALF + tq:] = kn_ref[...]
    vext[0:A_HALF] = vp_ref[...]
    gather(vext, A_HALF, v_ref)
    vext[A_HALF + tq:] = vn_ref[...]

    lane = lax.broadcasted_iota(jnp.int32, (ATT_SB, LANES), 1)
    low = lane < A_HEAD_DIM
    nt = (((1,), (1,)), ((), ()))

    def sub_block(jq, carry):
        r0 = pl.multiple_of(jq * ATT_SB, ATT_SB)
        kbase = i * tq + jq * ATT_SB - A_HALF
        variant = (kbase < 0).astype(jnp.int32) + 2 * (kbase + ATT_KW > sub).astype(jnp.int32)
        vws, scores = [], []
        for p in range(A_HEADS // 2):
            ls = slice(p * LANES, (p + 1) * LANES)
            qp = qbuf[pl.ds(r0, ATT_SB), ls]
            kw = kext[pl.ds(r0, ATT_KW), ls]
            vws.append(vext[pl.ds(r0, ATT_KW), ls])
            zero = jnp.zeros_like(qp)
            scores.append(lax.dot_general(jnp.where(low, qp, zero), kw, nt, preferred_element_type=F32))
            scores.append(lax.dot_general(jnp.where(low, zero, qp), kw, nt, preferred_element_type=F32))
        probs, inv_ls = [], []
        st = jnp.zeros((ATT_SB, LANES), F32)
        for h, s in enumerate(scores):
            t = s + bm_ref[variant * A_HEADS + h]
            m = jnp.max(t, axis=-1, keepdims=True)
            pe = jnp.exp2(t - m)
            l = jnp.sum(pe, axis=-1, keepdims=True)
            probs.append(pe.astype(BF16))
            inv_ls.append(1.0 / l)
            st = jnp.where(lane == h, (m + jnp.log2(l)) * LN2, st)
        pvs = [jnp.dot(pe, vws[h // 2], preferred_element_type=F32) * inv_l
               for h, (pe, inv_l) in enumerate(zip(probs, inv_ls))]
        for p in range(A_HEADS // 2):
            obuf[pl.ds(r0, ATT_SB), p * LANES:(p + 1) * LANES] = (
                jnp.where(low, pvs[2 * p], pvs[2 * p + 1]).astype(BF16))
        stbuf[pl.ds(r0, ATT_SB), :] = st
        return carry

    lax.fori_loop(0, tq // ATT_SB, sub_block, 0)

    if n_runs is None:
        o_ref[...] = obuf[...]
        st_ref[...] = stbuf[...]
    else:
        for c in range(n_runs):
            o_ref[c] = obuf[c * run:(c + 1) * run]
            st_ref[c] = stbuf[c * run:(c + 1) * run]


def _attn_bias(slopes, dilation):
    qi = jnp.arange(ATT_SB)[:, None]
    ci = jnp.arange(ATT_KW)[None, :]
    delta = ci - A_HALF - qi
    dist = (jnp.abs(delta) * dilation).astype(F32)
    bias = -slopes[:, None, None] * dist[None] * LOG2E
    band = jnp.abs(delta) <= A_HALF
    tables = []
    for v in range(4):
        ok = band
        if v & 1:
            ok = ok & (ci >= A_HALF)
        if v & 2:
            ok = ok & (ci < ATT_KW - A_HALF)
        tables.append(jnp.where(ok[None], bias, NEG))
    return jnp.concatenate(tables, axis=0)


def _attention_group(proj, bsz, seq, g, dilation, bm):
    sub = seq // dilation
    run = PERM_ROWS // dilation
    tq = min(ATT_TQ, sub)
    nq = sub // tq
    n_tiles = seq // PERM_ROWS
    view = proj.reshape(bsz, n_tiles, dilation, run, PROJ_W)
    cq = (COL_A + g * A_QKV_W) // A_WIDTH
    ck = cq + 1
    cv = cq + 2
    halo_per_run = run // A_HALF
    n_halo = sub // A_HALF
    halo_per_tq = tq // A_HALF

    if tq >= run:
        n_runs = tq // run

        def main(width, c):
            return pl.BlockSpec((None, n_runs, None, run, width), lambda b, r, i: (b, i, r, 0, c))
    else:
        n_runs = None
        per_run = run // tq

        def main(width, c):
            return pl.BlockSpec((None, None, None, tq, width),
                                lambda b, r, i: (b, i // per_run, r, i % per_run, c))

    def halo(c, which):
        def index(b, r, i):
            if which == "prev":
                hb = jnp.maximum(i * halo_per_tq - 1, 0)
            else:
                hb = jnp.minimum((i + 1) * halo_per_tq, n_halo - 1)
            return (b, hb // halo_per_run, r, hb % halo_per_run, c)
        return pl.BlockSpec((None, None, None, A_HALF, A_WIDTH), index)

    o, st = pl.pallas_call(
        functools.partial(_attn_kernel, tq=tq, sub=sub, n_runs=n_runs, run=run),
        grid=(bsz, dilation, nq),
        in_specs=[main(A_WIDTH, cq), halo(ck, "prev"), main(A_WIDTH, ck), halo(ck, "next"),
                  halo(cv, "prev"), main(A_WIDTH, cv), halo(cv, "next"),
                  _resident((4 * A_HEADS, ATT_SB, ATT_KW))],
        out_specs=[main(A_WIDTH, 0), main(LANES, 0)],
        out_shape=[jax.ShapeDtypeStruct((bsz, n_tiles, dilation, run, A_WIDTH), BF16),
                   jax.ShapeDtypeStruct((bsz, n_tiles, dilation, run, LANES), F32)],
        scratch_shapes=[pltpu.VMEM((tq, A_WIDTH), BF16),
                        pltpu.VMEM((tq + 2 * A_HALF, A_WIDTH), BF16),
                        pltpu.VMEM((tq + 2 * A_HALF, A_WIDTH), BF16),
                        pltpu.VMEM((tq, A_WIDTH), BF16),
                        pltpu.VMEM((tq, LANES), F32)],
        compiler_params=_cparams("parallel", "parallel", "arbitrary"),
        name=f"attn_d{dilation}",
    )(view, view, view, view, view, view, view, bm)
    n_perm = bsz * n_tiles
    return (o.reshape(n_perm, dilation, run, A_WIDTH), st.reshape(n_perm, dilation, run, LANES))


def _gla_kernel(q_ref, k_ref, v_ref, r_ref, gl_ref, wgf_ref, wgb_ref, bf_ref, bb_ref, gn_ref,
                o_ref, oacc, sf, sb, *, seq):
    c = B_CHUNK
    u = GLA_UNROLL
    n_chunks = seq // c
    sf[...] = jnp.zeros_like(sf)
    sb[...] = jnp.zeros_like(sb)
    ri = lax.broadcasted_iota(jnp.int32, (c, c), 0)
    ci = lax.broadcasted_iota(jnp.int32, (c, c), 1)
    lane = lax.broadcasted_iota(jnp.int32, (2 * c, LANES), 1)
    low = lane < c
    tril = (ci <= ri).astype(BF16)
    tril2 = jnp.concatenate([tril, tril], axis=1)
    kpad = jnp.zeros((c, B_DK), BF16)
    vpad = jnp.zeros((c, B_DV), BF16)
    ri2 = lax.broadcasted_iota(jnp.int32, (c, 2 * c), 0)
    ci2 = lax.broadcasted_iota(jnp.int32, (c, 2 * c), 1)
    nt = (((1,), (1,)), ((), ()))

    def log2_decay(z):
        e = jnp.exp2(jnp.abs(z) * (-LOG2E))
        return jnp.minimum(z, 0.0) * (LOG2E / B_GATE_TAU) - jnp.log2(1.0 + e) * (1.0 / B_GATE_TAU)

    def prefix_sum(la):
        hi = la.astype(BF16)
        lo = (la - hi.astype(F32)).astype(BF16)
        return jnp.dot(tril2, jnp.concatenate([hi, lo], axis=0), preferred_element_type=F32)

    def decay(la, cs, forward):
        tot = cs[c - 1:c, :]
        if forward:
            return cs, jnp.exp2(tot - cs), jnp.exp2(tot)
        return tot - cs + la, jnp.exp2(cs - la), jnp.exp2(tot)

    def body(nb, finish):
        rows_f = [pl.multiple_of((nb * u + j) * c, c) for j in range(u)]
        rows_b = [pl.multiple_of((n_chunks - 1 - nb * u - j) * c, c) for j in range(u)]
        chains = [(r, True) for r in rows_f] + [(r, False) for r in rows_b]
        zs = [jnp.dot(gl_ref[0, pl.ds(r, c), :], (wgf_ref if fw else wgb_ref)[0],
                      preferred_element_type=F32) + (bf_ref if fw else bb_ref)[0]
              for r, fw in chains]
        las = [log2_decay(z) for z in zs]
        css = [prefix_sum(la) for la in las]
        qes, kes, kds, decs = [], [], [], []
        for (r, fw), la, cs in zip(chains, las, css):
            bq, ekd, dec = decay(la, cs, fw)
            q = q_ref[0, pl.ds(r, c), :].astype(F32)
            k = k_ref[0, pl.ds(r, c), :].astype(F32)
            qes.append((q * jnp.exp2(bq)).astype(BF16))
            kes.append((k * jnp.exp2(-bq)).astype(BF16))
            kds.append(k * ekd)
            decs.append(dec)
        atts = [lax.dot_general(qe, jnp.concatenate([ke, kpad], axis=0), nt, preferred_element_type=F32)
                for qe, ke in zip(qes, kes)]
        atts = [jnp.where((ci2 <= ri2) if fw else ((ci2 > ri2) & (ci2 < c)), a, 0.0).astype(BF16)
                for (_, fw), a in zip(chains, atts)]
        vs = [v_ref[0, pl.ds(r, c), :] for r, _ in chains]
        ups_f, ups_b = [], []
        for j in range(u):
            kdt = jnp.concatenate([kds[j], kds[u + j]], axis=0).T
            vst = jnp.concatenate([vs[j], vs[u + j]], axis=0)
            ups_f.append(jnp.dot(jnp.where(low, kdt, 0.0).astype(BF16), vst, preferred_element_type=F32))
            ups_b.append(jnp.dot(jnp.where(low, 0.0, kdt).astype(BF16), vst, preferred_element_type=F32))
        states = []
        for st_ref, ups, ds in ((sf, ups_f, decs[:u]), (sb, ups_b, decs[u:])):
            st = st_ref[...]
            for up, dec in zip(ups, ds):
                states.append(st.astype(BF16))
                dcol = jnp.broadcast_to(dec, (B_DK, B_DK)).T
                st = st * jnp.concatenate([dcol, dcol], axis=1) + up
            st_ref[...] = st
        outs = [jnp.dot(jnp.concatenate([qe, a], axis=1), jnp.concatenate([st, v, vpad], axis=0),
                        preferred_element_type=F32)
                for a, v, qe, st in zip(atts, vs, qes, states)]
        for (r, _), o in zip(chains, outs):
            if finish:
                y = _rms(o + oacc[pl.ds(r, c), :], gn_ref[...])
                rr = r_ref[0, pl.ds(r, c), :].astype(F32)
                o_ref[0, pl.ds(r, c), :] = (y * (rr * _sigmoid(rr))).astype(BF16)
            else:
                oacc[pl.ds(r, c), :] = o

    def first_half(nb, carry):
        body(nb, False)
        return carry

    def second_half(nb, carry):
        body(nb, True)
        return carry

    n_bodies = n_chunks // u
    lax.fori_loop(0, n_bodies // 2, first_half, 0)
    lax.fori_loop(n_bodies // 2, n_bodies, second_half, 0)


def _gla(proj, bsz, seq, wgf, wgb, bgf, bgb, gn):
    view = proj.reshape(bsz, seq, PROJ_W)

    def cols(width, col0):
        return pl.BlockSpec((1, seq, width), lambda b, h: (b, 0, col0 // width + h))

    def per_head(shape):
        return pl.BlockSpec((1,) + shape, lambda b, h: (h, 0, 0))

    out = pl.pallas_call(
        functools.partial(_gla_kernel, seq=seq),
        grid=(bsz, B_HEADS),
        in_specs=[cols(B_DK, COL_QB), cols(B_DK, COL_KB), cols(B_DV, COL_VB), cols(B_DV, COL_RB),
                  pl.BlockSpec((1, seq, LANES), lambda b, h: (b, 0, COL_GL // LANES)),
                  per_head((LANES, B_DK)), per_head((LANES, B_DK)),
                  per_head((1, B_DK)), per_head((1, B_DK)),
                  pl.BlockSpec((1, B_DV), lambda b, h: (0, 0))],
        out_specs=pl.BlockSpec((1, seq, B_DV), lambda b, h: (b, 0, h)),
        out_shape=jax.ShapeDtypeStruct((bsz, seq, B_VW), BF16),
        scratch_shapes=[pltpu.VMEM((seq, B_DV), F32),
                        pltpu.VMEM((B_DK, B_DV), F32),
                        pltpu.VMEM((B_DK, B_DV), F32)],
        compiler_params=_cparams("parallel", "arbitrary"),
        name="gla",
    )(view, view, view, view, view, wgf, wgb, bgf, bgb, gn)
    return out.reshape(bsz * seq, B_VW)


def _merge_kernel(o1_ref, o2_ref, o3_ref, s1_ref, s2_ref, s3_ref, ob_ref, ga_ref, gb_ref, x_ref,
                  e_ref, wa_ref, wb_ref, wo_ref, out_ref, on2, on3, sn2, sn3):
    n_slabs = A_WIDTH // LANES
    for (_, d), o_ref, s_ref, on, sn in ((A_GROUPS[1], o2_ref, s2_ref, on2, sn2),
                                        (A_GROUPS[2], o3_ref, s3_ref, on3, sn3)):
        n = ROW_TM // d
        for r in range(d):
            sn[pl.ds(r, n, stride=d), :] = s_ref[r]
            for s in range(n_slabs):
                on[s, pl.ds(r, n, stride=d), :] = o_ref[r, :, s * LANES:(s + 1) * LANES].astype(F32)

    lses = (s1_ref[...], sn2[...], sn3[...])
    m = jnp.maximum(jnp.maximum(lses[0], lses[1]), lses[2])
    es = [jnp.exp(s - m) for s in lses]
    inv = 1.0 / (es[0] + es[1] + es[2])
    ws = [jnp.dot((e * inv).astype(BF16), e_ref[...], preferred_element_type=F32) for e in es]
    slabs = []
    for s in range(n_slabs):
        ls = slice(s * LANES, (s + 1) * LANES)
        slabs.append(o1_ref[:, ls].astype(F32) * ws[0][:, ls]
                     + on2[s] * ws[1][:, ls] + on3[s] * ws[2][:, ls])
    oa = jnp.concatenate(slabs, axis=1)
    ya = jnp.dot(oa.astype(BF16), wa_ref[...], preferred_element_type=F32)
    yb = jnp.dot(ob_ref[...], wb_ref[...], preferred_element_type=F32)
    merged = _sigmoid(ga_ref[...].astype(F32)) * ya + _sigmoid(gb_ref[...].astype(F32)) * yb
    out_ref[...] = x_ref[...] + jnp.dot(merged.astype(BF16), wo_ref[...], preferred_element_type=F32)


def _merge(oas, sts, ob, proj, x, expand, wa, wb, wo):
    t = x.shape[0]
    per_perm = PERM_ROWS // ROW_TM

    def rows(width, cblk=0):
        return pl.BlockSpec((ROW_TM, width), lambda i: (i, cblk))

    def perm(width, d):
        return pl.BlockSpec((None, d, ROW_TM // d, width), lambda i: (i // per_perm, 0, i % per_perm, 0))

    d2, d3 = A_GROUPS[1][1], A_GROUPS[2][1]
    return pl.pallas_call(
        _merge_kernel,
        grid=(t // ROW_TM,),
        in_specs=[rows(A_WIDTH), perm(A_WIDTH, d2), perm(A_WIDTH, d3),
                  rows(LANES), perm(LANES, d2), perm(LANES, d3),
                  rows(B_VW), rows(D_MODEL, COL_GA // D_MODEL), rows(D_MODEL, COL_GB // D_MODEL),
                  rows(D_MODEL),
                  _resident((LANES, A_WIDTH)), _resident((A_WIDTH, D_MODEL)),
                  _resident((B_VW, D_MODEL)), _resident((D_MODEL, D_MODEL))],
        out_specs=rows(D_MODEL),
        out_shape=jax.ShapeDtypeStruct((t, D_MODEL), F32),
        scratch_shapes=[pltpu.VMEM((A_WIDTH // LANES, ROW_TM, LANES), F32),
                        pltpu.VMEM((A_WIDTH // LANES, ROW_TM, LANES), F32),
                        pltpu.VMEM((ROW_TM, LANES), F32),
                        pltpu.VMEM((ROW_TM, LANES), F32)],
        compiler_params=_cparams("parallel"),
        name="merge",
    )(oas[0].reshape(t, A_WIDTH), oas[1], oas[2], sts[0].reshape(t, LANES), sts[1], sts[2],
      ob, proj, proj, x, expand, wa, wb, wo)


def _ffn_kernel(x_ref, g_ref, w1_ref, w2_ref, fg_ref, out_ref, *, final):
    x = x_ref[...]
    h = _rms(x, g_ref[...]).astype(BF16)
    acc = x
    for c in range(D_FF // D_MODEL):
        sl = slice(c * D_MODEL, (c + 1) * D_MODEL)
        u = jnp.maximum(jnp.dot(h, w1_ref[:, sl], preferred_element_type=F32), 0.0)
        acc = acc + jnp.dot((u * u).astype(BF16), w2_ref[sl, :], preferred_element_type=F32)
    if final:
        acc = _rms(acc, fg_ref[...])
    out_ref[...] = acc


def _ffn(x, g, w1, w2, fg, final):
    t = x.shape[0]
    return pl.pallas_call(
        functools.partial(_ffn_kernel, final=final),
        grid=(t // ROW_TM,),
        in_specs=[pl.BlockSpec((ROW_TM, D_MODEL), lambda i: (i, 0)),
                  _resident((1, D_MODEL)), _resident((D_MODEL, D_FF)), _resident((D_FF, D_MODEL)),
                  _resident((1, D_MODEL))],
        out_specs=pl.BlockSpec((ROW_TM, D_MODEL), lambda i: (i, 0)),
        out_shape=jax.ShapeDtypeStruct((t, D_MODEL), F32),
        compiler_params=_cparams("parallel"),
        name="ffn_final" if final else "ffn",
    )(x, g, w1, w2, fg)


def _prep_layer(l, norm_mix_g, w_in, wgf, bgf, wgb, bgb, gla_norm_g, w_branch_a, w_branch_b, w_out,
                norm_ffn_g, w_ff1, w_ff2):
    w = w_in[l]
    n_groups = len(A_GROUPS)
    a_end = 3 * n_groups * A_WIDTH
    wa = w[:, :a_end].reshape(D_MODEL, 3, n_groups, A_WIDTH)
    wa = wa * jnp.array([A_HEAD_DIM ** -0.5 * LOG2E, 1.0, 1.0], w.dtype)[None, :, None, None]
    wa = wa.transpose(0, 2, 1, 3).reshape(D_MODEL, a_end)
    gl0 = a_end + 2 * B_KW + 2 * B_VW
    gl1 = gl0 + 2 * B_GATE_RANK
    wqb = w[:, a_end:a_end + B_KW] * (B_DK ** -0.5)
    w_fused = jnp.concatenate(
        [wa, wqb, w[:, a_end + B_KW:gl1], jnp.zeros((D_MODEL, GL_PAD - 2 * B_GATE_RANK), w.dtype), w[:, gl1:]],
        axis=1)

    def gate_map(wg, row0):
        per_head = wg.reshape(B_GATE_RANK, B_HEADS, B_DK).transpose(1, 0, 2)
        return jnp.pad(per_head, ((0, 0), (row0, LANES - row0 - B_GATE_RANK), (0, 0))).astype(BF16)

    return dict(
        g_mix=norm_mix_g[l].reshape(1, D_MODEL),
        w_in=w_fused.astype(BF16),
        wgf=gate_map(wgf[l], 0),
        wgb=gate_map(wgb[l], B_GATE_RANK),
        bgf=bgf[l].reshape(B_HEADS, 1, B_DK),
        bgb=bgb[l].reshape(B_HEADS, 1, B_DK),
        gn=gla_norm_g[l].reshape(1, B_DV),
        wa=w_branch_a[l].astype(BF16),
        wb=w_branch_b[l].astype(BF16),
        wo=w_out[l].astype(BF16),
        g_ffn=norm_ffn_g[l].reshape(1, D_MODEL),
        w1=w_ff1[l].astype(BF16),
        w2=w_ff2[l].astype(BF16),
    )


def _trunk(x, layers, biases, expand, final_g):
    bsz, seq, _ = x.shape
    xf = x.reshape(bsz * seq, D_MODEL)
    for li, p in enumerate(layers):
        proj = _inproj(xf, p["g_mix"], p["w_in"])
        oas, sts = [], []
        for g, (_, dilation) in enumerate(A_GROUPS):
            o, st = _attention_group(proj, bsz, seq, g, dilation, biases[g])
            oas.append(o)
            sts.append(st)
        ob = _gla(proj, bsz, seq, p["wgf"], p["wgb"], p["bgf"], p["bgb"], p["gn"])
        x1 = _merge(oas, sts, ob, proj, xf, expand, p["wa"], p["wb"], p["wo"])
        xf = _ffn(x1, p["g_ffn"], p["w1"], p["w2"], final_g, final=(li == len(layers) - 1))
    return xf.reshape(bsz, seq, D_MODEL)


def kernel(x_prompt, x_sample, norm_mix_g, w_in, gla_w_gate_fwd, gla_b_gate_fwd, gla_w_gate_bwd,
           gla_b_gate_bwd, gla_norm_g, w_branch_a, w_branch_b, w_out, norm_ffn_g, w_ff1, w_ff2,
           final_norm_g):
    layers = [_prep_layer(l, norm_mix_g, w_in, gla_w_gate_fwd, gla_b_gate_fwd, gla_w_gate_bwd,
                          gla_b_gate_bwd, gla_norm_g, w_branch_a, w_branch_b, w_out, norm_ffn_g,
                          w_ff1, w_ff2) for l in range(DEPTH)]
    n_heads = len(A_GROUPS) * A_HEADS
    slopes = jnp.exp2(-8.0 * jnp.arange(1, n_heads + 1, dtype=F32) / n_heads)
    biases = [_attn_bias(slopes[g * A_HEADS:(g + 1) * A_HEADS], dilation)
              for g, (_, dilation) in enumerate(A_GROUPS)]
    expand = (jnp.arange(LANES)[:, None] == jnp.arange(A_WIDTH)[None, :] // A_HEAD_DIM).astype(BF16)
    final_g = final_norm_g.reshape(1, D_MODEL)
    return (_trunk(x_prompt, layers, biases, expand, final_g),
            _trunk(x_sample, layers, biases, expand, final_g))
```

```python
import functools

import jax
import jax.numpy as jnp
from jax import lax
from jax.experimental import pallas as pl
from jax.experimental.pallas import tpu as pltpu

F32 = jnp.float32
BF16 = jnp.bfloat16

D_MODEL = 1024
DEPTH = 2
A_GROUPS = ((128, 1), (512, 4), (2048, 16))
A_HEADS = 8
A_HEAD_DIM = 64
A_WIDTH = A_HEADS * A_HEAD_DIM
A_QKV_W = 3 * A_WIDTH
A_HALF = 64
B_HEADS = 4
B_DK = 128
B_DV = 256
B_KW = B_HEADS * B_DK
B_VW = B_HEADS * B_DV
B_GATE_RANK = 16
B_GATE_TAU = 16.0
B_CHUNK = 64
D_FF = 4 * D_MODEL
EPS = 1e-6
NEG = -1e30
LOG2E = 1.4426950408889634
LANES = 128

COL_A = 0
COL_QB = COL_A + len(A_GROUPS) * A_QKV_W
COL_KB = COL_QB + B_KW
COL_VB = COL_KB + B_KW
COL_RB = COL_VB + B_VW
COL_GL = COL_RB + B_VW
GL_PAD = 512
COL_GA = COL_GL + GL_PAD
COL_GB = COL_GA + D_MODEL
PROJ_W = COL_GB + D_MODEL

VMEM_LIMIT_BYTES = 56 * 1024 * 1024

PERM_ROWS = 1024
IN_TM = PERM_ROWS
IN_TN = 2048
IN_SUB = 512
ROW_TM = 512
ATT_TQ = 512
ATT_SB = 128
ATT_KW = ATT_SB + 2 * A_HALF
ATT_STAT_W = 2 * LANES
GLA_UNROLL = 8


def _cparams(*sem):
    return pltpu.CompilerParams(dimension_semantics=sem, vmem_limit_bytes=VMEM_LIMIT_BYTES)


def _resident(shape):
    nd = len(shape)
    return pl.BlockSpec(shape, lambda *_: (0,) * nd, pipeline_mode=pl.Buffered(1))


def _rms(x, g):
    return x * lax.rsqrt(jnp.mean(x * x, axis=-1, keepdims=True) + EPS) * g


def _sigmoid(t):
    return 1.0 / (1.0 + jnp.exp(-t))


def _inproj_kernel(x_ref, g_ref, w_ref, o_ref, hf_ref, hs_ref):
    j = pl.program_id(1)

    @pl.when(j == 0)
    def _():
        h = _rms(x_ref[...], g_ref[...])
        hs_ref[0] = h.astype(BF16)
        d1, d2 = A_GROUPS[1][1], A_GROUPS[2][1]
        n1, n2, ratio = IN_TM // d1, IN_TM // d2, d2 // d1
        for s in range(D_MODEL // LANES):
            sl = slice(s * LANES, (s + 1) * LANES)
            hf_ref[0, s] = h[:, sl]
            for r in range(d1):
                part = hf_ref[0, s, pl.ds(r, n1, stride=d1), :]
                hf_ref[1, s, r * n1:(r + 1) * n1, :] = part
                hs_ref[1, r * n1:(r + 1) * n1, sl] = part.astype(BF16)
            for r in range(d2):
                r1, q = r % d1, r // d1
                hs_ref[2, r * n2:(r + 1) * n2, sl] = (
                    hf_ref[1, s, pl.ds(r1 * n1 + q, n2, stride=ratio), :].astype(BF16))

    chunks_per_group = A_QKV_W // IN_SUB
    for c in range(IN_TN // IN_SUB):
        cc = j * (IN_TN // IN_SUB) + c
        k = jnp.where(cc < len(A_GROUPS) * chunks_per_group, cc // chunks_per_group, 0)
        sl = slice(c * IN_SUB, (c + 1) * IN_SUB)
        o_ref[:, sl] = jnp.dot(hs_ref[k], w_ref[:, sl], preferred_element_type=F32).astype(BF16)


def _inproj(x, g, w):
    t = x.shape[0]
    return pl.pallas_call(
        _inproj_kernel,
        grid=(t // IN_TM, PROJ_W // IN_TN),
        in_specs=[
            pl.BlockSpec((IN_TM, D_MODEL), lambda i, j: (i, 0)),
            pl.BlockSpec((1, D_MODEL), lambda i, j: (0, 0)),
            pl.BlockSpec((D_MODEL, IN_TN), lambda i, j: (0, j)),
        ],
        out_specs=pl.BlockSpec((IN_TM, IN_TN), lambda i, j: (i, j)),
        out_shape=jax.ShapeDtypeStruct((t, PROJ_W), BF16),
        scratch_shapes=[pltpu.VMEM((2, D_MODEL // LANES, IN_TM, LANES), F32),
                        pltpu.VMEM((len(A_GROUPS), IN_TM, D_MODEL), BF16)],
        compiler_params=_cparams("parallel", "arbitrary"),
        name="inproj",
    )(x, g, w)


def _attn_kernel(q_ref, kp_ref, k_ref, kn_ref, vp_ref, v_ref, vn_ref, bm_ref, o_ref, st_ref,
                 *, tq, sub, n_runs, run):
    i = pl.program_id(2)
    lane = lax.broadcasted_iota(jnp.int32, (ATT_SB, LANES), 1)
    low = lane < A_HEAD_DIM
    nt = (((1,), (1,)), ((), ()))

    def pieces(a, b):
        out, r = [], a
        while r < b:
            if n_runs is None:
                e = b
                out.append((None, r, e))
            else:
                c = r // run
                e = min(b, (c + 1) * run)
                out.append((c, r - c * run, e - c * run))
            r = e
        return out

    def read(ref, a, b, ls):
        parts = [ref[s:e, ls] if c is None else ref[c, s:e, ls] for c, s, e in pieces(a, b)]
        return parts[0] if len(parts) == 1 else jnp.concatenate(parts, axis=0)

    def window(prev_ref, ref, next_ref, a, ls):
        parts = []
        if a < 0:
            parts.append(prev_ref[:, ls])
        lo, hi = max(a, 0), min(a + ATT_KW, tq)
        parts.append(read(ref, lo, hi, ls))
        if a + ATT_KW > tq:
            parts.append(next_ref[:, ls])
        return jnp.concatenate(parts, axis=0)

    def write(ref, a, ls, val):
        off = 0
        for c, s, e in pieces(a, a + ATT_SB):
            if c is None:
                ref[s:e, ls] = val[off:off + e - s]
            else:
                ref[c, s:e, ls] = val[off:off + e - s]
            off += e - s

    for jq in range(tq // ATT_SB):
        r0 = jq * ATT_SB
        kbase = i * tq + r0 - A_HALF
        variant = (kbase < 0).astype(jnp.int32) + 2 * (kbase + ATT_KW > sub).astype(jnp.int32)
        vws, scores = [], []
        for p in range(A_HEADS // 2):
            ls = slice(p * LANES, (p + 1) * LANES)
            qp = read(q_ref, r0, r0 + ATT_SB, ls)
            kw = window(kp_ref, k_ref, kn_ref, r0 - A_HALF, ls)
            vws.append(window(vp_ref, v_ref, vn_ref, r0 - A_HALF, ls))
            zero = jnp.zeros_like(qp)
            scores.append(lax.dot_general(jnp.where(low, qp, zero), kw, nt, preferred_element_type=F32))
            scores.append(lax.dot_general(jnp.where(low, zero, qp), kw, nt, preferred_element_type=F32))
        probs = []
        st_m = jnp.zeros((ATT_SB, LANES), F32)
        st_l = jnp.ones((ATT_SB, LANES), F32)
        for h, s in enumerate(scores):
            t = s + bm_ref[variant * A_HEADS + h]
            m = jnp.max(t, axis=-1, keepdims=True)
            pe = jnp.exp2(t - m)
            probs.append(pe.astype(BF16))
            st_m = jnp.where(lane == h, m, st_m)
            st_l = jnp.where(lane == h, jnp.sum(pe, axis=-1, keepdims=True), st_l)
        pvs = [jnp.dot(pe, vws[h // 2], preferred_element_type=F32) for h, pe in enumerate(probs)]
        for p in range(A_HEADS // 2):
            write(o_ref, r0, slice(p * LANES, (p + 1) * LANES),
                  jnp.where(low, pvs[2 * p], pvs[2 * p + 1]).astype(BF16))
        write(st_ref, r0, slice(0, LANES), st_m)
        write(st_ref, r0, slice(LANES, 2 * LANES), st_l)


def _attn_bias(slopes, dilation):
    qi = jnp.arange(ATT_SB)[:, None]
    ci = jnp.arange(ATT_KW)[None, :]
    delta = ci - A_HALF - qi
    dist = (jnp.abs(delta) * dilation).astype(F32)
    bias = -slopes[:, None, None] * dist[None] * LOG2E
    band = jnp.abs(delta) <= A_HALF
    tables = []
    for v in range(4):
        ok = band
        if v & 1:
            ok = ok & (ci >= A_HALF)
        if v & 2:
            ok = ok & (ci < ATT_KW - A_HALF)
        tables.append(jnp.where(ok[None], bias, NEG))
    return jnp.concatenate(tables, axis=0)


def _attention_group(proj, bsz, seq, g, dilation, bm):
    sub = seq // dilation
    run = PERM_ROWS // dilation
    tq = min(ATT_TQ, sub)
    nq = sub // tq
    n_tiles = seq // PERM_ROWS
    view = proj.reshape(bsz, n_tiles, dilation, run, PROJ_W)
    cq = (COL_A + g * A_QKV_W) // A_WIDTH
    ck = cq + 1
    cv = cq + 2
    halo_per_run = run // A_HALF
    n_halo = sub // A_HALF
    halo_per_tq = tq // A_HALF

    if tq >= run:
        n_runs = tq // run

        def main(width, c):
            return pl.BlockSpec((None, n_runs, None, run, width), lambda b, r, i: (b, i, r, 0, c))
    else:
        n_runs = None
        per_run = run // tq

        def main(width, c):
            return pl.BlockSpec((None, None, None, tq, width),
                                lambda b, r, i: (b, i // per_run, r, i % per_run, c))

    def halo(c, which):
        def index(b, r, i):
            if which == "prev":
                hb = jnp.maximum(i * halo_per_tq - 1, 0)
            else:
                hb = jnp.minimum((i + 1) * halo_per_tq, n_halo - 1)
            return (b, hb // halo_per_run, r, hb % halo_per_run, c)
        return pl.BlockSpec((None, None, None, A_HALF, A_WIDTH), index)

    o, st = pl.pallas_call(
        functools.partial(_attn_kernel, tq=tq, sub=sub, n_runs=n_runs, run=run),
        grid=(bsz, dilation, nq),
        in_specs=[main(A_WIDTH, cq), halo(ck, "prev"), main(A_WIDTH, ck), halo(ck, "next"),
                  halo(cv, "prev"), main(A_WIDTH, cv), halo(cv, "next"),
                  _resident((4 * A_HEADS, ATT_SB, ATT_KW))],
        out_specs=[main(A_WIDTH, 0), main(ATT_STAT_W, 0)],
        out_shape=[jax.ShapeDtypeStruct((bsz, n_tiles, dilation, run, A_WIDTH), BF16),
                   jax.ShapeDtypeStruct((bsz, n_tiles, dilation, run, ATT_STAT_W), F32)],
        compiler_params=_cparams("parallel", "parallel", "arbitrary"),
        name=f"attn_d{dilation}",
    )(view, view, view, view, view, view, view, bm)
    n_perm = bsz * n_tiles
    return (o.reshape(n_perm, dilation, run, A_WIDTH), st.reshape(n_perm, dilation, run, ATT_STAT_W))


def _gla_kernel(q_ref, k_ref, v_ref, r_ref, gl_ref, wgf_ref, wgb_ref, bf_ref, bb_ref, gn_ref,
                o_ref, oacc, sf, sb, *, seq):
    c = B_CHUNK
    u = GLA_UNROLL
    n_chunks = seq // c
    sf[...] = jnp.zeros_like(sf)
    sb[...] = jnp.zeros_like(sb)
    ri = lax.broadcasted_iota(jnp.int32, (c, c), 0)
    ci = lax.broadcasted_iota(jnp.int32, (c, c), 1)
    lane = lax.broadcasted_iota(jnp.int32, (2 * c, LANES), 1)
    low = lane < c
    tril = (ci <= ri).astype(BF16)
    tril2 = jnp.concatenate([tril, tril], axis=1)
    kpad = jnp.zeros((c, B_DK), BF16)
    vpad = jnp.zeros((c, B_DV), BF16)
    ri2 = lax.broadcasted_iota(jnp.int32, (c, 2 * c), 0)
    ci2 = lax.broadcasted_iota(jnp.int32, (c, 2 * c), 1)
    nt = (((1,), (1,)), ((), ()))

    def log2_decay(z):
        e = jnp.exp2(jnp.abs(z) * (-LOG2E))
        return jnp.minimum(z, 0.0) * (LOG2E / B_GATE_TAU) - jnp.log2(1.0 + e) * (1.0 / B_GATE_TAU)

    def prefix_sum(la):
        hi = la.astype(BF16)
        lo = (la - hi.astype(F32)).astype(BF16)
        return jnp.dot(tril2, jnp.concatenate([hi, lo], axis=0), preferred_element_type=F32)

    def decay(la, cs, forward):
        tot = cs[c - 1:c, :]
        if forward:
            return cs, jnp.exp2(tot - cs), jnp.exp2(tot)
        return tot - cs + la, jnp.exp2(cs - la), jnp.exp2(tot)

    def body(nb, finish):
        rows_f = [pl.multiple_of((nb * u + j) * c, c) for j in range(u)]
        rows_b = [pl.multiple_of((n_chunks - 1 - nb * u - j) * c, c) for j in range(u)]
        chains = [(r, True) for r in rows_f] + [(r, False) for r in rows_b]
        zs = [jnp.dot(gl_ref[0, pl.ds(r, c), :], (wgf_ref if fw else wgb_ref)[0],
                      preferred_element_type=F32) + (bf_ref if fw else bb_ref)[0]
              for r, fw in chains]
        las = [log2_decay(z) for z in zs]
        css = [prefix_sum(la) for la in las]
        qes, kes, kds, decs = [], [], [], []
        for (r, fw), la, cs in zip(chains, las, css):
            bq, ekd, dec = decay(la, cs, fw)
            q = q_ref[0, pl.ds(r, c), :].astype(F32)
            k = k_ref[0, pl.ds(r, c), :].astype(F32)
            qes.append((q * jnp.exp2(bq)).astype(BF16))
            kes.append((k * jnp.exp2(-bq)).astype(BF16))
            kds.append(k * ekd)
            decs.append(dec)
        atts = [lax.dot_general(qe, jnp.concatenate([ke, kpad], axis=0), nt, preferred_element_type=F32)
                for qe, ke in zip(qes, kes)]
        atts = [jnp.where((ci2 <= ri2) if fw else ((ci2 > ri2) & (ci2 < c)), a, 0.0).astype(BF16)
                for (_, fw), a in zip(chains, atts)]
        vs = [v_ref[0, pl.ds(r, c), :] for r, _ in chains]
        ups_f, ups_b = [], []
        for j in range(u):
            kdt = jnp.concatenate([kds[j], kds[u + j]], axis=0).T
            vst = jnp.concatenate([vs[j], vs[u + j]], axis=0)
            ups_f.append(jnp.dot(jnp.where(low, kdt, 0.0).astype(BF16), vst, preferred_element_type=F32))
            ups_b.append(jnp.dot(jnp.where(low, 0.0, kdt).astype(BF16), vst, preferred_element_type=F32))
        states = []
        for st_ref, ups, ds in ((sf, ups_f, decs[:u]), (sb, ups_b, decs[u:])):
            st = st_ref[...]
            for up, dec in zip(ups, ds):
                states.append(st.astype(BF16))
                dcol = jnp.broadcast_to(dec, (B_DK, B_DK)).T
                st = st * jnp.concatenate([dcol, dcol], axis=1) + up
            st_ref[...] = st
        outs = [jnp.dot(jnp.concatenate([qe, a], axis=1), jnp.concatenate([st, v, vpad], axis=0),
                        preferred_element_type=F32)
                for a, v, qe, st in zip(atts, vs, qes, states)]
        for (r, _), o in zip(chains, outs):
            if finish:
                y = _rms(o + oacc[pl.ds(r, c), :], gn_ref[...])
                rr = r_ref[0, pl.ds(r, c), :].astype(F32)
                o_ref[0, pl.ds(r, c), :] = (y * (rr * _sigmoid(rr))).astype(BF16)
            else:
                oacc[pl.ds(r, c), :] = o

    def first_half(nb, carry):
        body(nb, False)
        return carry

    def second_half(nb, carry):
        body(nb, True)
        return carry

    n_bodies = n_chunks // u
    lax.fori_loop(0, n_bodies // 2, first_half, 0)
    lax.fori_loop(n_bodies // 2, n_bodies, second_half, 0)


def _gla(proj, bsz, seq, wgf, wgb, bgf, bgb, gn):
    view = proj.reshape(bsz, seq, PROJ_W)

    def cols(width, col0):
        return pl.BlockSpec((1, seq, width), lambda b, h: (b, 0, col0 // width + h))

    def per_head(shape):
        return pl.BlockSpec((1,) + shape, lambda b, h: (h, 0, 0))

    out = pl.pallas_call(
        functools.partial(_gla_kernel, seq=seq),
        grid=(bsz, B_HEADS),
        in_specs=[cols(B_DK, COL_QB), cols(B_DK, COL_KB), cols(B_DV, COL_VB), cols(B_DV, COL_RB),
                  pl.BlockSpec((1, seq, LANES), lambda b, h: (b, 0, COL_GL // LANES)),
                  per_head((LANES, B_DK)), per_head((LANES, B_DK)),
                  per_head((1, B_DK)), per_head((1, B_DK)),
                  pl.BlockSpec((1, B_DV), lambda b, h: (0, 0))],
        out_specs=pl.BlockSpec((1, seq, B_DV), lambda b, h: (b, 0, h)),
        out_shape=jax.ShapeDtypeStruct((bsz, seq, B_VW), BF16),
        scratch_shapes=[pltpu.VMEM((seq, B_DV), F32),
                        pltpu.VMEM((B_DK, B_DV), F32),
                        pltpu.VMEM((B_DK, B_DV), F32)],
        compiler_params=_cparams("parallel", "arbitrary"),
        name="gla",
    )(view, view, view, view, view, wgf, wgb, bgf, bgb, gn)
    return out.reshape(bsz * seq, B_VW)


def _merge_kernel(o1_ref, o2_ref, o3_ref, s1_ref, s2_ref, s3_ref, ob_ref, ga_ref, gb_ref, x_ref,
                  e_ref, wa_ref, wb_ref, wo_ref, out_ref, on2, on3, sn2, sn3):
    n_slabs = A_WIDTH // LANES
    for (_, d), o_ref, s_ref, on, sn in ((A_GROUPS[1], o2_ref, s2_ref, on2, sn2),
                                        (A_GROUPS[2], o3_ref, s3_ref, on3, sn3)):
        n = ROW_TM // d
        for r in range(d):
            for half in range(ATT_STAT_W // LANES):
                sn[half, pl.ds(r, n, stride=d), :] = s_ref[r, :, half * LANES:(half + 1) * LANES]
            for s in range(n_slabs):
                on[s, pl.ds(r, n, stride=d), :] = o_ref[r, :, s * LANES:(s + 1) * LANES].astype(F32)

    ms = (s1_ref[:, :LANES], sn2[0], sn3[0])
    ls = (s1_ref[:, LANES:], sn2[1], sn3[1])
    top = jnp.maximum(jnp.maximum(ms[0], ms[1]), ms[2])
    es = [jnp.exp2(m - top) for m in ms]
    inv = 1.0 / (es[0] * ls[0] + es[1] * ls[1] + es[2] * ls[2])
    ws = [jnp.dot((e * inv).astype(BF16), e_ref[...], preferred_element_type=F32) for e in es]
    slabs = []
    for s in range(n_slabs):
        sl = slice(s * LANES, (s + 1) * LANES)
        slabs.append(o1_ref[:, sl].astype(F32) * ws[0][:, sl]
                     + on2[s] * ws[1][:, sl] + on3[s] * ws[2][:, sl])
    oa = jnp.concatenate(slabs, axis=1)
    ya = jnp.dot(oa.astype(BF16), wa_ref[...], preferred_element_type=F32)
    yb = jnp.dot(ob_ref[...], wb_ref[...], preferred_element_type=F32)
    merged = _sigmoid(ga_ref[...].astype(F32)) * ya + _sigmoid(gb_ref[...].astype(F32)) * yb
    out_ref[...] = x_ref[...] + jnp.dot(merged.astype(BF16), wo_ref[...], preferred_element_type=F32)


def _merge(oas, sts, ob, proj, x, expand, wa, wb, wo):
    t = x.shape[0]
    per_perm = PERM_ROWS // ROW_TM

    def rows(width, cblk=0):
        return pl.BlockSpec((ROW_TM, width), lambda i: (i, cblk))

    def perm(width, d):
        return pl.BlockSpec((None, d, ROW_TM // d, width), lambda i: (i // per_perm, 0, i % per_perm, 0))

    d2, d3 = A_GROUPS[1][1], A_GROUPS[2][1]
    return pl.pallas_call(
        _merge_kernel,
        grid=(t // ROW_TM,),
        in_specs=[rows(A_WIDTH), perm(A_WIDTH, d2), perm(A_WIDTH, d3),
                  rows(ATT_STAT_W), perm(ATT_STAT_W, d2), perm(ATT_STAT_W, d3),
                  rows(B_VW), rows(D_MODEL, COL_GA // D_MODEL), rows(D_MODEL, COL_GB // D_MODEL),
                  rows(D_MODEL),
                  _resident((LANES, A_WIDTH)), _resident((A_WIDTH, D_MODEL)),
                  _resident((B_VW, D_MODEL)), _resident((D_MODEL, D_MODEL))],
        out_specs=rows(D_MODEL),
        out_shape=jax.ShapeDtypeStruct((t, D_MODEL), F32),
        scratch_shapes=[pltpu.VMEM((A_WIDTH // LANES, ROW_TM, LANES), F32),
                        pltpu.VMEM((A_WIDTH // LANES, ROW_TM, LANES), F32),
                        pltpu.VMEM((ATT_STAT_W // LANES, ROW_TM, LANES), F32),
                        pltpu.VMEM((ATT_STAT_W // LANES, ROW_TM, LANES), F32)],
        compiler_params=_cparams("parallel"),
        name="merge",
    )(oas[0].reshape(t, A_WIDTH), oas[1], oas[2], sts[0].reshape(t, ATT_STAT_W), sts[1], sts[2],
      ob, proj, proj, x, expand, wa, wb, wo)


def _ffn_kernel(x_ref, g_ref, w1_ref, w2_ref, fg_ref, out_ref, *, final):
    x = x_ref[...]
    h = _rms(x, g_ref[...]).astype(BF16)
    acc = x
    for c in range(D_FF // D_MODEL):
        sl = slice(c * D_MODEL, (c + 1) * D_MODEL)
        u = jnp.maximum(jnp.dot(h, w1_ref[:, sl], preferred_element_type=F32), 0.0)
        acc = acc + jnp.dot((u * u).astype(BF16), w2_ref[sl, :], preferred_element_type=F32)
    if final:
        acc = _rms(acc, fg_ref[...])
    out_ref[...] = acc


def _ffn(x, g, w1, w2, fg, final):
    t = x.shape[0]
    return pl.pallas_call(
        functools.partial(_ffn_kernel, final=final),
        grid=(t // ROW_TM,),
        in_specs=[pl.BlockSpec((ROW_TM, D_MODEL), lambda i: (i, 0)),
                  _resident((1, D_MODEL)), _resident((D_MODEL, D_FF)), _resident((D_FF, D_MODEL)),
                  _resident((1, D_MODEL))],
        out_specs=pl.BlockSpec((ROW_TM, D_MODEL), lambda i: (i, 0)),
        out_shape=jax.ShapeDtypeStruct((t, D_MODEL), F32),
        compiler_params=_cparams("parallel"),
        name="ffn_final" if final else "ffn",
    )(x, g, w1, w2, fg)


def _prep_layer(l, norm_mix_g, w_in, wgf, bgf, wgb, bgb, gla_norm_g, w_branch_a, w_branch_b, w_out,
                norm_ffn_g, w_ff1, w_ff2):
    w = w_in[l]
    n_groups = len(A_GROUPS)
    a_end = 3 * n_groups * A_WIDTH
    wa = w[:, :a_end].reshape(D_MODEL, 3, n_groups, A_WIDTH)
    wa = wa * jnp.array([A_HEAD_DIM ** -0.5 * LOG2E, 1.0, 1.0], w.dtype)[None, :, None, None]
    wa = wa.transpose(0, 2, 1, 3).reshape(D_MODEL, a_end)
    gl0 = a_end + 2 * B_KW + 2 * B_VW
    gl1 = gl0 + 2 * B_GATE_RANK
    wqb = w[:, a_end:a_end + B_KW] * (B_DK ** -0.5)
    w_fused = jnp.concatenate(
        [wa, wqb, w[:, a_end + B_KW:gl1], jnp.zeros((D_MODEL, GL_PAD - 2 * B_GATE_RANK), w.dtype), w[:, gl1:]],
        axis=1)

    def gate_map(wg, row0):
        per_head = wg.reshape(B_GATE_RANK, B_HEADS, B_DK).transpose(1, 0, 2)
        return jnp.pad(per_head, ((0, 0), (row0, LANES - row0 - B_GATE_RANK), (0, 0))).astype(BF16)

    return dict(
        g_mix=norm_mix_g[l].reshape(1, D_MODEL),
        w_in=w_fused.astype(BF16),
        wgf=gate_map(wgf[l], 0),
        wgb=gate_map(wgb[l], B_GATE_RANK),
        bgf=bgf[l].reshape(B_HEADS, 1, B_DK),
        bgb=bgb[l].reshape(B_HEADS, 1, B_DK),
        gn=gla_norm_g[l].reshape(1, B_DV),
        wa=w_branch_a[l].astype(BF16),
        wb=w_branch_b[l].astype(BF16),
        wo=w_out[l].astype(BF16),
        g_ffn=norm_ffn_g[l].reshape(1, D_MODEL),
        w1=w_ff1[l].astype(BF16),
        w2=w_ff2[l].astype(BF16),
    )


def _trunk(x, layers, biases, expand, final_g):
    bsz, seq, _ = x.shape
    xf = x.reshape(bsz * seq, D_MODEL)
    for li, p in enumerate(layers):
        proj = _inproj(xf, p["g_mix"], p["w_in"])
        oas, sts = [], []
        for g, (_, dilation) in enumerate(A_GROUPS):
            o, st = _attention_group(proj, bsz, seq, g, dilation, biases[g])
            oas.append(o)
            sts.append(st)
        ob = _gla(proj, bsz, seq, p["wgf"], p["wgb"], p["bgf"], p["bgb"], p["gn"])
        x1 = _merge(oas, sts, ob, proj, xf, expand, p["wa"], p["wb"], p["wo"])
        xf = _ffn(x1, p["g_ffn"], p["w1"], p["w2"], final_g, final=(li == len(layers) - 1))
    return xf.reshape(bsz, seq, D_MODEL)


def kernel(x_prompt, x_sample, norm_mix_g, w_in, gla_w_gate_fwd, gla_b_gate_fwd, gla_w_gate_bwd,
           gla_b_gate_bwd, gla_norm_g, w_branch_a, w_branch_b, w_out, norm_ffn_g, w_ff1, w_ff2,
           final_norm_g):
    layers = [_prep_layer(l, norm_mix_g, w_in, gla_w_gate_fwd, gla_b_gate_fwd, gla_w_gate_bwd,
                          gla_b_gate_bwd, gla_norm_g, w_branch_a, w_branch_b, w_out, norm_ffn_g,
                          w_ff1, w_ff2) for l in range(DEPTH)]
    n_heads = len(A_GROUPS) * A_HEADS
    slopes = jnp.exp2(-8.0 * jnp.arange(1, n_heads + 1, dtype=F32) / n_heads)
    biases = [_attn_bias(slopes[g * A_HEADS:(g + 1) * A_HEADS], dilation)
              for g, (_, dilation) in enumerate(A_GROUPS)]
    expand = (jnp.arange(LANES)[:, None] == jnp.arange(A_WIDTH)[None, :] // A_HEAD_DIM).astype(BF16)
    final_g = final_norm_g.reshape(1, D_MODEL)
    return (_trunk(x_prompt, layers, biases, expand, final_g),
            _trunk(x_sample, layers, biases, expand, final_g))
```

```python
import functools

import jax
import jax.numpy as jnp
from jax import lax
from jax.experimental import pallas as pl
from jax.experimental.pallas import tpu as pltpu

F32 = jnp.float32
BF16 = jnp.bfloat16

D_MODEL = 1024
DEPTH = 2
A_GROUPS = ((128, 1), (512, 4), (2048, 16))
A_HEADS = 8
A_HEAD_DIM = 64
A_WIDTH = A_HEADS * A_HEAD_DIM
A_QKV_W = 3 * A_WIDTH
A_HALF = 64
B_HEADS = 4
B_DK = 128
B_DV = 256
B_KW = B_HEADS * B_DK
B_VW = B_HEADS * B_DV
B_GATE_RANK = 16
B_GATE_TAU = 16.0
B_CHUNK = 64
D_FF = 4 * D_MODEL
EPS = 1e-6
NEG = -1e30
LOG2E = 1.4426950408889634
LANES = 128

COL_A = 0
COL_QB = COL_A + len(A_GROUPS) * A_QKV_W
COL_KB = COL_QB + B_KW
COL_VB = COL_KB + B_KW
COL_RB = COL_VB + B_VW
COL_GL = COL_RB + B_VW
GL_PAD = 512
COL_GA = COL_GL + GL_PAD
COL_GB = COL_GA + D_MODEL
PROJ_W = COL_GB + D_MODEL

VMEM_LIMIT_BYTES = 56 * 1024 * 1024

PERM_ROWS = 1024
IN_TM = PERM_ROWS
IN_TN = 2048
IN_SUB = 512
ROW_TM = 512
ATT_ROWS = 1024
ATT_SB = 128
ATT_KW = ATT_SB + 2 * A_HALF
GLA_UNROLL = 8


def _cparams(*sem):
    return pltpu.CompilerParams(dimension_semantics=sem, vmem_limit_bytes=VMEM_LIMIT_BYTES)


def _resident(shape):
    nd = len(shape)
    return pl.BlockSpec(shape, lambda *_: (0,) * nd, pipeline_mode=pl.Buffered(1))


def _rms(x, g):
    return x * lax.rsqrt(jnp.mean(x * x, axis=-1, keepdims=True) + EPS) * g


def _sigmoid(t):
    return 1.0 / (1.0 + jnp.exp(-t))


def _inproj_kernel(x_ref, g_ref, w_ref, o_ref, hf_ref, hs_ref):
    j = pl.program_id(1)

    @pl.when(j == 0)
    def _():
        h = _rms(x_ref[...], g_ref[...])
        hs_ref[0] = h.astype(BF16)
        d1, d2 = A_GROUPS[1][1], A_GROUPS[2][1]
        n1, n2, ratio = IN_TM // d1, IN_TM // d2, d2 // d1
        for s in range(D_MODEL // LANES):
            sl = slice(s * LANES, (s + 1) * LANES)
            hf_ref[0, s] = h[:, sl]
            for r in range(d1):
                part = hf_ref[0, s, pl.ds(r, n1, stride=d1), :]
                hf_ref[1, s, r * n1:(r + 1) * n1, :] = part
                hs_ref[1, r * n1:(r + 1) * n1, sl] = part.astype(BF16)
            for r in range(d2):
                r1, q = r % d1, r // d1
                hs_ref[2, r * n2:(r + 1) * n2, sl] = (
                    hf_ref[1, s, pl.ds(r1 * n1 + q, n2, stride=ratio), :].astype(BF16))

    n_groups = len(A_GROUPS)
    for c in range(IN_TN // IN_SUB):
        cc = j * (IN_TN // IN_SUB) + c
        k = jnp.where(cc < 3 * n_groups, cc % n_groups, 0)
        sl = slice(c * IN_SUB, (c + 1) * IN_SUB)
        o_ref[:, sl] = jnp.dot(hs_ref[k], w_ref[:, sl], preferred_element_type=F32).astype(BF16)


def _inproj(x, g, w):
    t = x.shape[0]
    return pl.pallas_call(
        _inproj_kernel,
        grid=(t // IN_TM, PROJ_W // IN_TN),
        in_specs=[
            pl.BlockSpec((IN_TM, D_MODEL), lambda i, j: (i, 0)),
            pl.BlockSpec((1, D_MODEL), lambda i, j: (0, 0)),
            pl.BlockSpec((D_MODEL, IN_TN), lambda i, j: (0, j)),
        ],
        out_specs=pl.BlockSpec((IN_TM, IN_TN), lambda i, j: (i, j)),
        out_shape=jax.ShapeDtypeStruct((t, PROJ_W), BF16),
        scratch_shapes=[pltpu.VMEM((2, D_MODEL // LANES, IN_TM, LANES), F32),
                        pltpu.VMEM((len(A_GROUPS), IN_TM, D_MODEL), BF16)],
        compiler_params=_cparams("parallel", "arbitrary"),
        name="inproj",
    )(x, g, w)


def _attn_kernel(q_ref, kp_ref, k_ref, kn_ref, vp_ref, v_ref, vn_ref, bm_ref, o_ref, st_ref,
                 *, tq, sub, run, n_res):
    i = pl.program_id(2)
    lane = lax.broadcasted_iota(jnp.int32, (ATT_SB, LANES), 1)
    low = lane < A_HEAD_DIM
    nt = (((1,), (1,)), ((), ()))

    def pieces(a, b):
        out, r = [], a
        while r < b:
            c = r // run
            e = min(b, (c + 1) * run)
            out.append((c, r - c * run, e - c * run))
            r = e
        return out

    def read(ref, rr, a, b, ls):
        parts = [ref[c, rr, s:e, ls] for c, s, e in pieces(a, b)]
        return parts[0] if len(parts) == 1 else jnp.concatenate(parts, axis=0)

    def window(prev_ref, ref, next_ref, rr, a, ls):
        parts = []
        if a < 0:
            parts.append(prev_ref[rr, :, ls])
        parts.append(read(ref, rr, max(a, 0), min(a + ATT_KW, tq), ls))
        if a + ATT_KW > tq:
            parts.append(next_ref[rr, :, ls])
        return jnp.concatenate(parts, axis=0)

    def write(ref, rr, a, ls, val):
        off = 0
        for c, s, e in pieces(a, a + ATT_SB):
            ref[c, rr, s:e, ls] = val[off:off + e - s]
            off += e - s

    for rr, jq in [(rr, jq) for rr in range(n_res) for jq in range(tq // ATT_SB)]:
        r0 = jq * ATT_SB
        kbase = i * tq + r0 - A_HALF
        variant = (kbase < 0).astype(jnp.int32) + 2 * (kbase + ATT_KW > sub).astype(jnp.int32)
        vws, scores = [], []
        for p in range(A_HEADS // 2):
            ls = slice(p * LANES, (p + 1) * LANES)
            qp = read(q_ref, rr, r0, r0 + ATT_SB, ls)
            kw = window(kp_ref, k_ref, kn_ref, rr, r0 - A_HALF, ls)
            vws.append(window(vp_ref, v_ref, vn_ref, rr, r0 - A_HALF, ls))
            zero = jnp.zeros_like(qp)
            scores.append(lax.dot_general(jnp.where(low, qp, zero), kw, nt, preferred_element_type=F32))
            scores.append(lax.dot_general(jnp.where(low, zero, qp), kw, nt, preferred_element_type=F32))
        probs = []
        st = jnp.ones((ATT_SB, LANES), F32)
        for h, s in enumerate(scores):
            t = s + bm_ref[variant * A_HEADS + h]
            m = jnp.max(t, axis=-1, keepdims=True)
            pe = jnp.exp2(t - m)
            probs.append(pe.astype(BF16))
            st = jnp.where(lane == h, m, st)
            st = jnp.where(lane == A_HEADS + h, jnp.sum(pe, axis=-1, keepdims=True), st)
        pvs = [jnp.dot(pe, vws[h // 2], preferred_element_type=F32) for h, pe in enumerate(probs)]
        for p in range(A_HEADS // 2):
            write(o_ref, rr, r0, slice(p * LANES, (p + 1) * LANES),
                  jnp.where(low, pvs[2 * p], pvs[2 * p + 1]).astype(BF16))
        write(st_ref, rr, r0, slice(0, LANES), st)


def _attn_bias(slopes, dilation):
    qi = jnp.arange(ATT_SB)[:, None]
    ci = jnp.arange(ATT_KW)[None, :]
    delta = ci - A_HALF - qi
    dist = (jnp.abs(delta) * dilation).astype(F32)
    bias = -slopes[:, None, None] * dist[None] * LOG2E
    band = jnp.abs(delta) <= A_HALF
    tables = []
    for v in range(4):
        ok = band
        if v & 1:
            ok = ok & (ci >= A_HALF)
        if v & 2:
            ok = ok & (ci < ATT_KW - A_HALF)
        tables.append(jnp.where(ok[None], bias, NEG))
    return jnp.concatenate(tables, axis=0)


def _attention_group(proj, bsz, seq, g, dilation, bm):
    sub = seq // dilation
    run = PERM_ROWS // dilation
    tq = min(ATT_ROWS, sub)
    assert tq % run == 0 and sub % tq == 0
    n_runs = tq // run
    n_res = min(dilation, ATT_ROWS // tq)
    n_tiles = seq // PERM_ROWS
    view = proj.reshape(bsz, n_tiles, dilation, run, PROJ_W)
    n_groups = len(A_GROUPS)
    cq = COL_A // A_WIDTH + g
    ck = cq + n_groups
    cv = ck + n_groups
    halo_per_run = run // A_HALF
    n_halo = sub // A_HALF
    halo_per_tq = tq // A_HALF

    def main(width, c):
        return pl.BlockSpec((None, n_runs, n_res, run, width), lambda b, r, i: (b, i, r, 0, c))

    def halo(c, which):
        def index(b, r, i):
            if which == "prev":
                hb = jnp.maximum(i * halo_per_tq - 1, 0)
            else:
                hb = jnp.minimum((i + 1) * halo_per_tq, n_halo - 1)
            return (b, hb // halo_per_run, r, hb % halo_per_run, c)
        return pl.BlockSpec((None, None, n_res, A_HALF, A_WIDTH), index)

    o, st = pl.pallas_call(
        functools.partial(_attn_kernel, tq=tq, sub=sub, run=run, n_res=n_res),
        grid=(bsz, dilation // n_res, sub // tq),
        in_specs=[main(A_WIDTH, cq), halo(ck, "prev"), main(A_WIDTH, ck), halo(ck, "next"),
                  halo(cv, "prev"), main(A_WIDTH, cv), halo(cv, "next"),
                  _resident((4 * A_HEADS, ATT_SB, ATT_KW))],
        out_specs=[main(A_WIDTH, 0), main(LANES, 0)],
        out_shape=[jax.ShapeDtypeStruct((bsz, n_tiles, dilation, run, A_WIDTH), BF16),
                   jax.ShapeDtypeStruct((bsz, n_tiles, dilation, run, LANES), F32)],
        compiler_params=_cparams("parallel", "parallel", "arbitrary"),
        name=f"attn_d{dilation}",
    )(view, view, view, view, view, view, view, bm)
    n_perm = bsz * n_tiles
    return (o.reshape(n_perm, dilation, run, A_WIDTH), st.reshape(n_perm, dilation, run, LANES))


def _gla_kernel(q_ref, k_ref, v_ref, r_ref, gl_ref, wgf_ref, wgb_ref, bf_ref, bb_ref, gn_ref,
                o_ref, oacc, sf, sb, *, seq):
    c = B_CHUNK
    u = GLA_UNROLL
    n_chunks = seq // c
    sf[...] = jnp.zeros_like(sf)
    sb[...] = jnp.zeros_like(sb)
    ri = lax.broadcasted_iota(jnp.int32, (c, c), 0)
    ci = lax.broadcasted_iota(jnp.int32, (c, c), 1)
    lane = lax.broadcasted_iota(jnp.int32, (2 * c, LANES), 1)
    low = lane < c
    tril = (ci <= ri).astype(BF16)
    tril2 = jnp.concatenate([tril, tril], axis=1)
    kpad = jnp.zeros((c, B_DK), BF16)
    vpad = jnp.zeros((c, B_DV), BF16)
    ri2 = lax.broadcasted_iota(jnp.int32, (c, 2 * c), 0)
    ci2 = lax.broadcasted_iota(jnp.int32, (c, 2 * c), 1)
    nt = (((1,), (1,)), ((), ()))

    def log2_decay(z):
        e = jnp.exp2(jnp.abs(z) * (-LOG2E))
        return (jnp.minimum(z, 0.0) - jnp.log(1.0 + e)) * (LOG2E / B_GATE_TAU)

    def prefix_sum(la):
        hi = la.astype(BF16)
        lo = (la - hi.astype(F32)).astype(BF16)
        return jnp.dot(tril2, jnp.concatenate([hi, lo], axis=0), preferred_element_type=F32)

    def decay(la, cs, forward):
        tot = cs[c - 1:c, :]
        if forward:
            return cs, jnp.exp2(tot - cs), jnp.exp2(tot)
        return tot - cs + la, jnp.exp2(cs - la), jnp.exp2(tot)

    def body(nb, finish):
        rows_f = [pl.multiple_of((nb * u + j) * c, c) for j in range(u)]
        rows_b = [pl.multiple_of((n_chunks - 1 - nb * u - j) * c, c) for j in range(u)]
        chains = [(r, True) for r in rows_f] + [(r, False) for r in rows_b]
        zs = [jnp.dot(gl_ref[0, pl.ds(r, c), :], (wgf_ref if fw else wgb_ref)[0],
                      preferred_element_type=F32) + (bf_ref if fw else bb_ref)[0]
              for r, fw in chains]
        las = [log2_decay(z) for z in zs]
        css = [prefix_sum(la) for la in las]
        qes, kes, kds, decs = [], [], [], []
        for (r, fw), la, cs in zip(chains, las, css):
            bq, ekd, dec = decay(la, cs, fw)
            q = q_ref[0, pl.ds(r, c), :].astype(F32)
            k = k_ref[0, pl.ds(r, c), :].astype(F32)
            qes.append((q * jnp.exp2(bq)).astype(BF16))
            kes.append((k * jnp.exp2(-bq)).astype(BF16))
            kds.append(k * ekd)
            decs.append(dec)
        atts = [lax.dot_general(qe, jnp.concatenate([ke, kpad], axis=0), nt, preferred_element_type=F32)
                for qe, ke in zip(qes, kes)]
        atts = [jnp.where((ci2 <= ri2) if fw else ((ci2 > ri2) & (ci2 < c)), a, 0.0).astype(BF16)
                for (_, fw), a in zip(chains, atts)]
        vs = [v_ref[0, pl.ds(r, c), :] for r, _ in chains]
        ups_f, ups_b = [], []
        for j in range(u):
            kdt = jnp.concatenate([kds[j], kds[u + j]], axis=0).T
            vst = jnp.concatenate([vs[j], vs[u + j]], axis=0)
            ups_f.append(jnp.dot(jnp.where(low, kdt, 0.0).astype(BF16), vst, preferred_element_type=F32))
            ups_b.append(jnp.dot(jnp.where(low, 0.0, kdt).astype(BF16), vst, preferred_element_type=F32))
        states = []
        for st_ref, ups, ds in ((sf, ups_f, decs[:u]), (sb, ups_b, decs[u:])):
            st = st_ref[...]
            for up, dec in zip(ups, ds):
                states.append(st.astype(BF16))
                dcol = jnp.broadcast_to(dec, (B_DK, B_DK)).T
                st = st * jnp.concatenate([dcol, dcol], axis=1) + up
            st_ref[...] = st
        outs = [jnp.dot(jnp.concatenate([qe, a], axis=1), jnp.concatenate([st, v, vpad], axis=0),
                        preferred_element_type=F32)
                for a, v, qe, st in zip(atts, vs, qes, states)]
        for (r, _), o in zip(chains, outs):
            if finish:
                y = _rms(o + oacc[pl.ds(r, c), :], gn_ref[...])
                rr = r_ref[0, pl.ds(r, c), :].astype(F32)
                o_ref[0, pl.ds(r, c), :] = (y * (rr * _sigmoid(rr))).astype(BF16)
            else:
                oacc[pl.ds(r, c), :] = o

    def first_half(nb, carry):
        body(nb, False)
        return carry

    def second_half(nb, carry):
        body(nb, True)
        return carry

    n_bodies = n_chunks // u
    lax.fori_loop(0, n_bodies // 2, first_half, 0)
    lax.fori_loop(n_bodies // 2, n_bodies, second_half, 0)


def _gla(proj, bsz, seq, wgf, wgb, bgf, bgb, gn):
    view = proj.reshape(bsz, seq, PROJ_W)

    def cols(width, col0):
        return pl.BlockSpec((1, seq, width), lambda b, h: (b, 0, col0 // width + h))

    def per_head(shape):
        return pl.BlockSpec((1,) + shape, lambda b, h: (h, 0, 0))

    out = pl.pallas_call(
        functools.partial(_gla_kernel, seq=seq),
        grid=(bsz, B_HEADS),
        in_specs=[cols(B_DK, COL_QB), cols(B_DK, COL_KB), cols(B_DV, COL_VB), cols(B_DV, COL_RB),
                  pl.BlockSpec((1, seq, LANES), lambda b, h: (b, 0, COL_GL // LANES)),
                  per_head((LANES, B_DK)), per_head((LANES, B_DK)),
                  per_head((1, B_DK)), per_head((1, B_DK)),
                  pl.BlockSpec((1, B_DV), lambda b, h: (0, 0))],
        out_specs=pl.BlockSpec((1, seq, B_DV), lambda b, h: (b, 0, h)),
        out_shape=jax.ShapeDtypeStruct((bsz, seq, B_VW), BF16),
        scratch_shapes=[pltpu.VMEM((seq, B_DV), F32),
                        pltpu.VMEM((B_DK, B_DV), F32),
                        pltpu.VMEM((B_DK, B_DV), F32)],
        compiler_params=_cparams("parallel", "arbitrary"),
        name="gla",
    )(view, view, view, view, view, wgf, wgb, bgf, bgb, gn)
    return out.reshape(bsz * seq, B_VW)


def _merge_kernel(o1_ref, o2_ref, o3_ref, s1_ref, s2_ref, s3_ref, ob_ref, ga_ref, gb_ref, x_ref,
                  e_ref, wa_ref, wb_ref, wo_ref, out_ref, on2, on3, sn2, sn3):
    n_slabs = A_WIDTH // LANES
    for (_, d), o_ref, s_ref, on, sn in ((A_GROUPS[1], o2_ref, s2_ref, on2, sn2),
                                        (A_GROUPS[2], o3_ref, s3_ref, on3, sn3)):
        n = ROW_TM // d
        for r in range(d):
            sn[pl.ds(r, n, stride=d), :] = s_ref[r]
            for s in range(n_slabs):
                on[s, pl.ds(r, n, stride=d), :] = o_ref[r, :, s * LANES:(s + 1) * LANES].astype(F32)

    ms = (s1_ref[...], sn2[...], sn3[...])
    ls = [pltpu.roll(m, LANES - A_HEADS, axis=1) for m in ms]
    top = jnp.maximum(jnp.maximum(ms[0], ms[1]), ms[2])
    es = [jnp.exp2(m - top) for m in ms]
    inv = 1.0 / (es[0] * ls[0] + es[1] * ls[1] + es[2] * ls[2])
    ws = [jnp.dot((e * inv).astype(BF16), e_ref[...], preferred_element_type=F32) for e in es]
    slabs = []
    for s in range(n_slabs):
        sl = slice(s * LANES, (s + 1) * LANES)
        slabs.append(o1_ref[:, sl].astype(F32) * ws[0][:, sl]
                     + on2[s] * ws[1][:, sl] + on3[s] * ws[2][:, sl])
    oa = jnp.concatenate(slabs, axis=1)
    ya = jnp.dot(oa.astype(BF16), wa_ref[...], preferred_element_type=F32)
    yb = jnp.dot(ob_ref[...], wb_ref[...], preferred_element_type=F32)
    merged = _sigmoid(ga_ref[...].astype(F32)) * ya + _sigmoid(gb_ref[...].astype(F32)) * yb
    out_ref[...] = x_ref[...] + jnp.dot(merged.astype(BF16), wo_ref[...], preferred_element_type=F32)


def _merge(oas, sts, ob, proj, x, expand, wa, wb, wo):
    t = x.shape[0]
    per_perm = PERM_ROWS // ROW_TM

    def rows(width, cblk=0):
        return pl.BlockSpec((ROW_TM, width), lambda i: (i, cblk))

    def perm(width, d):
        return pl.BlockSpec((None, d, ROW_TM // d, width), lambda i: (i // per_perm, 0, i % per_perm, 0))

    d2, d3 = A_GROUPS[1][1], A_GROUPS[2][1]
    return pl.pallas_call(
        _merge_kernel,
        grid=(t // ROW_TM,),
        in_specs=[rows(A_WIDTH), perm(A_WIDTH, d2), perm(A_WIDTH, d3),
                  rows(LANES), perm(LANES, d2), perm(LANES, d3),
                  rows(B_VW), rows(D_MODEL, COL_GA // D_MODEL), rows(D_MODEL, COL_GB // D_MODEL),
                  rows(D_MODEL),
                  _resident((LANES, A_WIDTH)), _resident((A_WIDTH, D_MODEL)),
                  _resident((B_VW, D_MODEL)), _resident((D_MODEL, D_MODEL))],
        out_specs=rows(D_MODEL),
        out_shape=jax.ShapeDtypeStruct((t, D_MODEL), F32),
        scratch_shapes=[pltpu.VMEM((A_WIDTH // LANES, ROW_TM, LANES), F32),
                        pltpu.VMEM((A_WIDTH // LANES, ROW_TM, LANES), F32),
                        pltpu.VMEM((ROW_TM, LANES), F32),
                        pltpu.VMEM((ROW_TM, LANES), F32)],
        compiler_params=_cparams("parallel"),
        name="merge",
    )(oas[0].reshape(t, A_WIDTH), oas[1], oas[2], sts[0].reshape(t, LANES), sts[1], sts[2],
      ob, proj, proj, x, expand, wa, wb, wo)


def _ffn_kernel(x_ref, g_ref, w1_ref, w2_ref, fg_ref, out_ref, *, final):
    x = x_ref[...]
    h = (x * g_ref[...]).astype(BF16)
    acc = jnp.zeros_like(x)
    for c in range(D_FF // D_MODEL):
        sl = slice(c * D_MODEL, (c + 1) * D_MODEL)
        u = jnp.maximum(jnp.dot(h, w1_ref[:, sl], preferred_element_type=F32), 0.0)
        acc = acc + jnp.dot((u * u).astype(BF16), w2_ref[sl, :], preferred_element_type=F32)
    acc = x + acc * (1.0 / (jnp.mean(x * x, axis=-1, keepdims=True) + EPS))
    if final:
        acc = _rms(acc, fg_ref[...])
    out_ref[...] = acc


def _ffn(x, g, w1, w2, fg, final):
    t = x.shape[0]
    return pl.pallas_call(
        functools.partial(_ffn_kernel, final=final),
        grid=(t // ROW_TM,),
        in_specs=[pl.BlockSpec((ROW_TM, D_MODEL), lambda i: (i, 0)),
                  _resident((1, D_MODEL)), _resident((D_MODEL, D_FF)), _resident((D_FF, D_MODEL)),
                  _resident((1, D_MODEL))],
        out_specs=pl.BlockSpec((ROW_TM, D_MODEL), lambda i: (i, 0)),
        out_shape=jax.ShapeDtypeStruct((t, D_MODEL), F32),
        compiler_params=_cparams("parallel"),
        name="ffn_final" if final else "ffn",
    )(x, g, w1, w2, fg)


def _prep_layer(l, norm_mix_g, w_in, wgf, bgf, wgb, bgb, gla_norm_g, w_branch_a, w_branch_b, w_out,
                norm_ffn_g, w_ff1, w_ff2):
    w = w_in[l]
    n_groups = len(A_GROUPS)
    a_end = 3 * n_groups * A_WIDTH
    gl1 = a_end + 2 * B_KW + 2 * B_VW + 2 * B_GATE_RANK
    col = jnp.arange(gl1)
    scale = jnp.where(col < n_groups * A_WIDTH, A_HEAD_DIM ** -0.5 * LOG2E,
                      jnp.where((col >= a_end) & (col < a_end + B_KW), B_DK ** -0.5, 1.0)).astype(w.dtype)
    w_fused = jnp.concatenate(
        [w[:, :gl1] * scale[None, :], jnp.zeros((D_MODEL, GL_PAD - 2 * B_GATE_RANK), w.dtype), w[:, gl1:]],
        axis=1)

    def gate_map(wg, row0):
        per_head = wg.reshape(B_GATE_RANK, B_HEADS, B_DK).transpose(1, 0, 2)
        return jnp.pad(per_head, ((0, 0), (row0, LANES - row0 - B_GATE_RANK), (0, 0))).astype(BF16)

    return dict(
        g_mix=norm_mix_g[l].reshape(1, D_MODEL),
        w_in=w_fused.astype(BF16),
        wgf=gate_map(wgf[l], 0),
        wgb=gate_map(wgb[l], B_GATE_RANK),
        bgf=bgf[l].reshape(B_HEADS, 1, B_DK),
        bgb=bgb[l].reshape(B_HEADS, 1, B_DK),
        gn=gla_norm_g[l].reshape(1, B_DV),
        wa=w_branch_a[l].astype(BF16),
        wb=w_branch_b[l].astype(BF16),
        wo=w_out[l].astype(BF16),
        g_ffn=norm_ffn_g[l].reshape(1, D_MODEL),
        w1=w_ff1[l].astype(BF16),
        w2=w_ff2[l].astype(BF16),
    )


def _trunk(x, layers, biases, expand, final_g):
    bsz, seq, _ = x.shape
    xf = x.reshape(bsz * seq, D_MODEL)
    for li, p in enumerate(layers):
        proj = _inproj(xf, p["g_mix"], p["w_in"])
        oas, sts = [], []
        for g, (_, dilation) in enumerate(A_GROUPS):
            o, st = _attention_group(proj, bsz, seq, g, dilation, biases[g])
            oas.append(o)
            sts.append(st)
        ob = _gla(proj, bsz, seq, p["wgf"], p["wgb"], p["bgf"], p["bgb"], p["gn"])
        x1 = _merge(oas, sts, ob, proj, xf, expand, p["wa"], p["wb"], p["wo"])
        xf = _ffn(x1, p["g_ffn"], p["w1"], p["w2"], final_g, final=(li == len(layers) - 1))
    return xf.reshape(bsz, seq, D_MODEL)


def kernel(x_prompt, x_sample, norm_mix_g, w_in, gla_w_gate_fwd, gla_b_gate_fwd, gla_w_gate_bwd,
           gla_b_gate_bwd, gla_norm_g, w_branch_a, w_branch_b, w_out, norm_ffn_g, w_ff1, w_ff2,
           final_norm_g):
    layers = [_prep_layer(l, norm_mix_g, w_in, gla_w_gate_fwd, gla_b_gate_fwd, gla_w_gate_bwd,
                          gla_b_gate_bwd, gla_norm_g, w_branch_a, w_branch_b, w_out, norm_ffn_g,
                          w_ff1, w_ff2) for l in range(DEPTH)]
    n_heads = len(A_GROUPS) * A_HEADS
    slopes = jnp.exp2(-8.0 * jnp.arange(1, n_heads + 1, dtype=F32) / n_heads)
    biases = [_attn_bias(slopes[g * A_HEADS:(g + 1) * A_HEADS], dilation)
              for g, (_, dilation) in enumerate(A_GROUPS)]
    expand = (jnp.arange(LANES)[:, None] == jnp.arange(A_WIDTH)[None, :] // A_HEAD_DIM).astype(BF16)
    final_g = final_norm_g.reshape(1, D_MODEL)
    return (_trunk(x_prompt, layers, biases, expand, final_g),
            _trunk(x_sample, layers, biases, expand, final_g))
```

```python
import functools

import jax
import jax.numpy as jnp
from jax import lax
from jax.experimental import pallas as pl
from jax.experimental.pallas import tpu as pltpu

F32 = jnp.float32
BF16 = jnp.bfloat16

D_MODEL = 1024
DEPTH = 2
A_GROUPS = ((128, 1), (512, 4), (2048, 16))
A_HEADS = 8
A_HEAD_DIM = 64
A_WIDTH = A_HEADS * A_HEAD_DIM
A_QKV_W = 3 * A_WIDTH
A_HALF = 64
B_HEADS = 4
B_DK = 128
B_DV = 256
B_KW = B_HEADS * B_DK
B_VW = B_HEADS * B_DV
B_GATE_RANK = 16
B_GATE_TAU = 16.0
B_CHUNK = 64
D_FF = 4 * D_MODEL
EPS = 1e-6
NEG = -1e30
LOG2E = 1.4426950408889634
LANES = 128

COL_A = 0
COL_QB = COL_A + len(A_GROUPS) * A_QKV_W
COL_KB = COL_QB + B_KW
COL_VB = COL_KB + B_KW
COL_RB = COL_VB + B_VW
COL_GL = COL_RB + B_VW
GL_PAD = 512
COL_GA = COL_GL + GL_PAD
COL_GB = COL_GA + D_MODEL
PROJ_W = COL_GB + D_MODEL

VMEM_LIMIT_BYTES = 56 * 1024 * 1024

PERM_ROWS = 1024
IN_TM = PERM_ROWS
IN_TN = 2560
IN_SUB = 512
ROW_TM = 512
ATT_ROWS = 1024
ATT_SB = 128
ATT_KW = ATT_SB + 2 * A_HALF
GLA_UNROLL = 16


def _cparams(*sem):
    return pltpu.CompilerParams(dimension_semantics=sem, vmem_limit_bytes=VMEM_LIMIT_BYTES)


def _resident(shape):
    nd = len(shape)
    return pl.BlockSpec(shape, lambda *_: (0,) * nd, pipeline_mode=pl.Buffered(1))


def _rms(x, g):
    return x * lax.rsqrt(jnp.mean(x * x, axis=-1, keepdims=True) + EPS) * g


def _sigmoid(t):
    return 1.0 / (1.0 + jnp.exp(-t))


def _inproj_kernel(x_ref, g_ref, w_ref, o_ref, hf_ref, hs_ref):
    j = pl.program_id(1)

    @pl.when(j == 0)
    def _():
        h = _rms(x_ref[...], g_ref[...])
        hs_ref[0] = h.astype(BF16)
        d1, d2 = A_GROUPS[1][1], A_GROUPS[2][1]
        n1, n2, ratio = IN_TM // d1, IN_TM // d2, d2 // d1
        for s in range(D_MODEL // LANES):
            sl = slice(s * LANES, (s + 1) * LANES)
            hf_ref[0, s] = h[:, sl]
            for r in range(d1):
                part = hf_ref[0, s, pl.ds(r, n1, stride=d1), :]
                hf_ref[1, s, r * n1:(r + 1) * n1, :] = part
                hs_ref[1, r * n1:(r + 1) * n1, sl] = part.astype(BF16)
            for r in range(d2):
                r1, q = r % d1, r // d1
                hs_ref[2, r * n2:(r + 1) * n2, sl] = (
                    hf_ref[1, s, pl.ds(r1 * n1 + q, n2, stride=ratio), :].astype(BF16))

    n_groups = len(A_GROUPS)
    for c in range(IN_TN // IN_SUB):
        cc = j * (IN_TN // IN_SUB) + c
        k = jnp.where(cc < 3 * n_groups, cc % n_groups, 0)
        sl = slice(c * IN_SUB, (c + 1) * IN_SUB)
        o_ref[:, sl] = jnp.dot(hs_ref[k], w_ref[:, sl], preferred_element_type=F32).astype(BF16)


def _inproj(x, g, w):
    t = x.shape[0]
    return pl.pallas_call(
        _inproj_kernel,
        grid=(t // IN_TM, PROJ_W // IN_TN),
        in_specs=[
            pl.BlockSpec((IN_TM, D_MODEL), lambda i, j: (i, 0)),
            pl.BlockSpec((1, D_MODEL), lambda i, j: (0, 0)),
            pl.BlockSpec((D_MODEL, IN_TN), lambda i, j: (0, j)),
        ],
        out_specs=pl.BlockSpec((IN_TM, IN_TN), lambda i, j: (i, j)),
        out_shape=jax.ShapeDtypeStruct((t, PROJ_W), BF16),
        scratch_shapes=[pltpu.VMEM((2, D_MODEL // LANES, IN_TM, LANES), F32),
                        pltpu.VMEM((len(A_GROUPS), IN_TM, D_MODEL), BF16)],
        compiler_params=_cparams("parallel", "arbitrary"),
        name="inproj",
    )(x, g, w)


def _attn_kernel(q_ref, kp_ref, k_ref, kn_ref, vp_ref, v_ref, vn_ref, bm_ref, o_ref, st_ref,
                 *, tq, sub, run, n_res):
    i = pl.program_id(2)
    lane = lax.broadcasted_iota(jnp.int32, (ATT_SB, LANES), 1)
    low = lane < A_HEAD_DIM
    nt = (((1,), (1,)), ((), ()))

    def pieces(a, b):
        out, r = [], a
        while r < b:
            c = r // run
            e = min(b, (c + 1) * run)
            out.append((c, r - c * run, e - c * run))
            r = e
        return out

    def read(ref, rr, a, b, ls):
        parts = [ref[c, rr, s:e, ls] for c, s, e in pieces(a, b)]
        return parts[0] if len(parts) == 1 else jnp.concatenate(parts, axis=0)

    def window(prev_ref, ref, next_ref, rr, a, ls):
        parts = []
        if a < 0:
            parts.append(prev_ref[rr, :, ls])
        parts.append(read(ref, rr, max(a, 0), min(a + ATT_KW, tq), ls))
        if a + ATT_KW > tq:
            parts.append(next_ref[rr, :, ls])
        return jnp.concatenate(parts, axis=0)

    def write(ref, rr, a, ls, val):
        off = 0
        for c, s, e in pieces(a, a + ATT_SB):
            ref[c, rr, s:e, ls] = val[off:off + e - s]
            off += e - s

    for rr, jq in [(rr, jq) for rr in range(n_res) for jq in range(tq // ATT_SB)]:
        r0 = jq * ATT_SB
        kbase = i * tq + r0 - A_HALF
        variant = (kbase < 0).astype(jnp.int32) + 2 * (kbase + ATT_KW > sub).astype(jnp.int32)
        vws, scores = [], []
        for p in range(A_HEADS // 2):
            ls = slice(p * LANES, (p + 1) * LANES)
            qp = read(q_ref, rr, r0, r0 + ATT_SB, ls)
            kw = window(kp_ref, k_ref, kn_ref, rr, r0 - A_HALF, ls)
            vws.append(window(vp_ref, v_ref, vn_ref, rr, r0 - A_HALF, ls))
            zero = jnp.zeros_like(qp)
            scores.append(lax.dot_general(jnp.where(low, qp, zero), kw, nt, preferred_element_type=F32))
            scores.append(lax.dot_general(jnp.where(low, zero, qp), kw, nt, preferred_element_type=F32))
        probs = []
        st = jnp.ones((ATT_SB, LANES), F32)
        for h, s in enumerate(scores):
            t = s + bm_ref[variant * A_HEADS + h]
            m = jnp.max(t, axis=-1, keepdims=True)
            pe = jnp.exp2(t - m)
            probs.append(pe.astype(BF16))
            st = jnp.where(lane == h, m, st)
            st = jnp.where(lane == A_HEADS + h, jnp.sum(pe, axis=-1, keepdims=True), st)
        pvs = [jnp.dot(pe, vws[h // 2], preferred_element_type=F32) for h, pe in enumerate(probs)]
        for p in range(A_HEADS // 2):
            write(o_ref, rr, r0, slice(p * LANES, (p + 1) * LANES),
                  jnp.where(low, pvs[2 * p], pvs[2 * p + 1]).astype(BF16))
        write(st_ref, rr, r0, slice(0, LANES), st)


def _attn_bias(slopes, dilation):
    qi = jnp.arange(ATT_SB)[:, None]
    ci = jnp.arange(ATT_KW)[None, :]
    delta = ci - A_HALF - qi
    dist = (jnp.abs(delta) * dilation).astype(F32)
    bias = -slopes[:, None, None] * dist[None] * LOG2E
    band = jnp.abs(delta) <= A_HALF
    tables = []
    for v in range(4):
        ok = band
        if v & 1:
            ok = ok & (ci >= A_HALF)
        if v & 2:
            ok = ok & (ci < ATT_KW - A_HALF)
        tables.append(jnp.where(ok[None], bias, NEG))
    return jnp.concatenate(tables, axis=0)


def _attention_group(proj, bsz, seq, g, dilation, bm):
    sub = seq // dilation
    run = PERM_ROWS // dilation
    tq = min(ATT_ROWS, sub)
    assert tq % run == 0 and sub % tq == 0
    n_runs = tq // run
    n_res = min(dilation, ATT_ROWS // tq)
    n_tiles = seq // PERM_ROWS
    view = proj.reshape(bsz, n_tiles, dilation, run, PROJ_W)
    n_groups = len(A_GROUPS)
    cq = COL_A // A_WIDTH + g
    ck = cq + n_groups
    cv = ck + n_groups
    halo_per_run = run // A_HALF
    n_halo = sub // A_HALF
    halo_per_tq = tq // A_HALF

    def main(width, c):
        return pl.BlockSpec((None, n_runs, n_res, run, width), lambda b, r, i: (b, i, r, 0, c))

    def halo(c, which):
        def index(b, r, i):
            if which == "prev":
                hb = jnp.maximum(i * halo_per_tq - 1, 0)
            else:
                hb = jnp.minimum((i + 1) * halo_per_tq, n_halo - 1)
            return (b, hb // halo_per_run, r, hb % halo_per_run, c)
        return pl.BlockSpec((None, None, n_res, A_HALF, A_WIDTH), index)

    o, st = pl.pallas_call(
        functools.partial(_attn_kernel, tq=tq, sub=sub, run=run, n_res=n_res),
        grid=(bsz, dilation // n_res, sub // tq),
        in_specs=[main(A_WIDTH, cq), halo(ck, "prev"), main(A_WIDTH, ck), halo(ck, "next"),
                  halo(cv, "prev"), main(A_WIDTH, cv), halo(cv, "next"),
                  _resident((4 * A_HEADS, ATT_SB, ATT_KW))],
        out_specs=[main(A_WIDTH, 0), main(LANES, 0)],
        out_shape=[jax.ShapeDtypeStruct((bsz, n_tiles, dilation, run, A_WIDTH), BF16),
                   jax.ShapeDtypeStruct((bsz, n_tiles, dilation, run, LANES), F32)],
        compiler_params=_cparams("parallel", "parallel", "arbitrary"),
        name=f"attn_d{dilation}",
    )(view, view, view, view, view, view, view, bm)
    n_perm = bsz * n_tiles
    return (o.reshape(n_perm, dilation, run, A_WIDTH), st.reshape(n_perm, dilation, run, LANES))


def _gla_kernel(q_ref, k_ref, v_ref, r_ref, gl_ref, wgf_ref, wgb_ref, bf_ref, bb_ref, gn_ref,
                o_ref, oacc, sf, sb, *, seq):
    c = B_CHUNK
    u = GLA_UNROLL
    n_chunks = seq // c
    sf[...] = jnp.zeros_like(sf)
    sb[...] = jnp.zeros_like(sb)
    ri = lax.broadcasted_iota(jnp.int32, (c, c), 0)
    ci = lax.broadcasted_iota(jnp.int32, (c, c), 1)
    lane = lax.broadcasted_iota(jnp.int32, (2 * c, LANES), 1)
    low = lane < c
    tril = (ci <= ri).astype(BF16)
    tril2 = jnp.concatenate([tril, tril], axis=1)
    kpad = jnp.zeros((c, B_DK), BF16)
    vpad = jnp.zeros((c, B_DV), BF16)
    ri2 = lax.broadcasted_iota(jnp.int32, (c, 2 * c), 0)
    ci2 = lax.broadcasted_iota(jnp.int32, (c, 2 * c), 1)
    nt = (((1,), (1,)), ((), ()))

    def log2_decay(z):
        e = jnp.exp2(jnp.abs(z) * (-LOG2E))
        return (jnp.minimum(z, 0.0) - jnp.log(1.0 + e)) * (LOG2E / B_GATE_TAU)

    def prefix_sum(la):
        hi = la.astype(BF16)
        lo = (la - hi.astype(F32)).astype(BF16)
        return jnp.dot(tril2, jnp.concatenate([hi, lo], axis=0), preferred_element_type=F32)

    def decay(la, cs, forward):
        tot = cs[c - 1:c, :]
        if forward:
            return cs, jnp.exp2(tot - cs), jnp.exp2(tot)
        return tot - cs + la, jnp.exp2(cs - la), jnp.exp2(tot)

    def body(nb, finish):
        rows_f = [pl.multiple_of((nb * u + j) * c, c) for j in range(u)]
        rows_b = [pl.multiple_of((n_chunks - 1 - nb * u - j) * c, c) for j in range(u)]
        chains = [(r, True) for r in rows_f] + [(r, False) for r in rows_b]
        zs = [jnp.dot(gl_ref[0, pl.ds(r, c), :], (wgf_ref if fw else wgb_ref)[0],
                      preferred_element_type=F32) + (bf_ref if fw else bb_ref)[0]
              for r, fw in chains]
        las = [log2_decay(z) for z in zs]
        css = [prefix_sum(la) for la in las]
        qes, kes, kds, decs = [], [], [], []
        for (r, fw), la, cs in zip(chains, las, css):
            bq, ekd, dec = decay(la, cs, fw)
            q = q_ref[0, pl.ds(r, c), :].astype(F32)
            k = k_ref[0, pl.ds(r, c), :].astype(F32)
            qes.append((q * jnp.exp2(bq)).astype(BF16))
            kes.append((k * jnp.exp2(-bq)).astype(BF16))
            kds.append(k * ekd)
            decs.append(dec)
        atts = [lax.dot_general(qe, jnp.concatenate([ke, kpad], axis=0), nt, preferred_element_type=F32)
                for qe, ke in zip(qes, kes)]
        atts = [jnp.where((ci2 <= ri2) if fw else ((ci2 > ri2) & (ci2 < c)), a, 0.0).astype(BF16)
                for (_, fw), a in zip(chains, atts)]
        vs = [v_ref[0, pl.ds(r, c), :] for r, _ in chains]
        ups_f, ups_b = [], []
        for j in range(u):
            kdt = jnp.concatenate([kds[j], kds[u + j]], axis=0).T
            vst = jnp.concatenate([vs[j], vs[u + j]], axis=0)
            ups_f.append(jnp.dot(jnp.where(low, kdt, 0.0).astype(BF16), vst, preferred_element_type=F32))
            ups_b.append(jnp.dot(jnp.where(low, 0.0, kdt).astype(BF16), vst, preferred_element_type=F32))
        states = []
        for st_ref, ups, ds in ((sf, ups_f, decs[:u]), (sb, ups_b, decs[u:])):
            st = st_ref[...]
            for up, dec in zip(ups, ds):
                states.append(st.astype(BF16))
                dcol = jnp.broadcast_to(dec, (B_DK, B_DK)).T
                st = st * jnp.concatenate([dcol, dcol], axis=1) + up
            st_ref[...] = st
        outs = [jnp.dot(jnp.concatenate([qe, a], axis=1), jnp.concatenate([st, v, vpad], axis=0),
                        preferred_element_type=F32)
                for a, v, qe, st in zip(atts, vs, qes, states)]
        for (r, _), o in zip(chains, outs):
            if finish:
                y = _rms(o + oacc[pl.ds(r, c), :], gn_ref[...])
                rr = r_ref[0, pl.ds(r, c), :].astype(F32)
                o_ref[0, pl.ds(r, c), :] = (y * (rr * _sigmoid(rr))).astype(BF16)
            else:
                oacc[pl.ds(r, c), :] = o

    def first_half(nb, carry):
        body(nb, False)
        return carry

    def second_half(nb, carry):
        body(nb, True)
        return carry

    n_bodies = n_chunks // u
    lax.fori_loop(0, n_bodies // 2, first_half, 0)
    lax.fori_loop(n_bodies // 2, n_bodies, second_half, 0)


def _gla(proj, bsz, seq, wgf, wgb, bgf, bgb, gn):
    view = proj.reshape(bsz, seq, PROJ_W)

    def cols(width, col0):
        return pl.BlockSpec((1, seq, width), lambda b, h: (b, 0, col0 // width + h))

    def per_head(shape):
        return pl.BlockSpec((1,) + shape, lambda b, h: (h, 0, 0))

    out = pl.pallas_call(
        functools.partial(_gla_kernel, seq=seq),
        grid=(bsz, B_HEADS),
        in_specs=[cols(B_DK, COL_QB), cols(B_DK, COL_KB), cols(B_DV, COL_VB), cols(B_DV, COL_RB),
                  pl.BlockSpec((1, seq, LANES), lambda b, h: (b, 0, COL_GL // LANES)),
                  per_head((LANES, B_DK)), per_head((LANES, B_DK)),
                  per_head((1, B_DK)), per_head((1, B_DK)),
                  pl.BlockSpec((1, B_DV), lambda b, h: (0, 0))],
        out_specs=pl.BlockSpec((1, seq, B_DV), lambda b, h: (b, 0, h)),
        out_shape=jax.ShapeDtypeStruct((bsz, seq, B_VW), BF16),
        scratch_shapes=[pltpu.VMEM((seq, B_DV), F32),
                        pltpu.VMEM((B_DK, B_DV), F32),
                        pltpu.VMEM((B_DK, B_DV), F32)],
        compiler_params=_cparams("parallel", "arbitrary"),
        name="gla",
    )(view, view, view, view, view, wgf, wgb, bgf, bgb, gn)
    return out.reshape(bsz * seq, B_VW)


def _merge_rows(o1_ref, o2_ref, o3_ref, s1_ref, s2_ref, s3_ref, ob_ref, ga_ref, gb_ref, x_ref,
                e_ref, wa_ref, wb_ref, wo_ref, on2, on3, sn2, sn3):
    n_slabs = A_WIDTH // LANES
    for (_, d), o_ref, s_ref, on, sn in ((A_GROUPS[1], o2_ref, s2_ref, on2, sn2),
                                        (A_GROUPS[2], o3_ref, s3_ref, on3, sn3)):
        n = ROW_TM // d
        for r in range(d):
            sn[pl.ds(r, n, stride=d), :] = s_ref[r]
            for s in range(n_slabs):
                on[s, pl.ds(r, n, stride=d), :] = o_ref[r, :, s * LANES:(s + 1) * LANES].astype(F32)

    ms = (s1_ref[...], sn2[...], sn3[...])
    ls = [pltpu.roll(m, LANES - A_HEADS, axis=1) for m in ms]
    top = jnp.maximum(jnp.maximum(ms[0], ms[1]), ms[2])
    es = [jnp.exp2(m - top) for m in ms]
    inv = 1.0 / (es[0] * ls[0] + es[1] * ls[1] + es[2] * ls[2])
    ws = [jnp.dot((e * inv).astype(BF16), e_ref[...], preferred_element_type=F32) for e in es]
    slabs = []
    for s in range(n_slabs):
        sl = slice(s * LANES, (s + 1) * LANES)
        slabs.append(o1_ref[:, sl].astype(F32) * ws[0][:, sl]
                     + on2[s] * ws[1][:, sl] + on3[s] * ws[2][:, sl])
    oa = jnp.concatenate(slabs, axis=1)
    ya = jnp.dot(oa.astype(BF16), wa_ref[...], preferred_element_type=F32)
    yb = jnp.dot(ob_ref[...], wb_ref[...], preferred_element_type=F32)
    merged = _sigmoid(ga_ref[...].astype(F32)) * ya + _sigmoid(gb_ref[...].astype(F32)) * yb
    return x_ref[...] + jnp.dot(merged.astype(BF16), wo_ref[...], preferred_element_type=F32)


def _mlp_rows(x, g_ref, w1_ref, w2_ref):
    h = (x * g_ref[...]).astype(BF16)
    acc = jnp.zeros_like(x)
    for c in range(D_FF // D_MODEL):
        sl = slice(c * D_MODEL, (c + 1) * D_MODEL)
        u = jnp.maximum(jnp.dot(h, w1_ref[:, sl], preferred_element_type=F32), 0.0)
        acc = acc + jnp.dot((u * u).astype(BF16), w2_ref[sl, :], preferred_element_type=F32)
    return x + acc * (1.0 / (jnp.mean(x * x, axis=-1, keepdims=True) + EPS))


def _tail_kernel(o1_ref, o2_ref, o3_ref, s1_ref, s2_ref, s3_ref, ob_ref, ga_ref, gb_ref, x_ref,
                 e_ref, wa_ref, wb_ref, wo_ref, g_ref, w1_ref, w2_ref, fg_ref, out_ref,
                 on2, on3, sn2, sn3, *, final):
    x1 = _merge_rows(o1_ref, o2_ref, o3_ref, s1_ref, s2_ref, s3_ref, ob_ref, ga_ref, gb_ref, x_ref,
                     e_ref, wa_ref, wb_ref, wo_ref, on2, on3, sn2, sn3)
    x2 = _mlp_rows(x1, g_ref, w1_ref, w2_ref)
    out_ref[...] = _rms(x2, fg_ref[...]) if final else x2


def _tail(oas, sts, ob, proj, x, expand, wa, wb, wo, g, w1, w2, fg, final):
    t = x.shape[0]
    per_perm = PERM_ROWS // ROW_TM

    def rows(width, cblk=0):
        return pl.BlockSpec((ROW_TM, width), lambda i: (i, cblk))

    def perm(width, d):
        return pl.BlockSpec((None, d, ROW_TM // d, width), lambda i: (i // per_perm, 0, i % per_perm, 0))

    d2, d3 = A_GROUPS[1][1], A_GROUPS[2][1]
    return pl.pallas_call(
        functools.partial(_tail_kernel, final=final),
        grid=(t // ROW_TM,),
        in_specs=[rows(A_WIDTH), perm(A_WIDTH, d2), perm(A_WIDTH, d3),
                  rows(LANES), perm(LANES, d2), perm(LANES, d3),
                  rows(B_VW), rows(D_MODEL, COL_GA // D_MODEL), rows(D_MODEL, COL_GB // D_MODEL),
                  rows(D_MODEL),
                  _resident((LANES, A_WIDTH)), _resident((A_WIDTH, D_MODEL)),
                  _resident((B_VW, D_MODEL)), _resident((D_MODEL, D_MODEL)),
                  _resident((1, D_MODEL)), _resident((D_MODEL, D_FF)), _resident((D_FF, D_MODEL)),
                  _resident((1, D_MODEL))],
        out_specs=rows(D_MODEL),
        out_shape=jax.ShapeDtypeStruct((t, D_MODEL), F32),
        scratch_shapes=[pltpu.VMEM((A_WIDTH // LANES, ROW_TM, LANES), F32),
                        pltpu.VMEM((A_WIDTH // LANES, ROW_TM, LANES), F32),
                        pltpu.VMEM((ROW_TM, LANES), F32),
                        pltpu.VMEM((ROW_TM, LANES), F32)],
        compiler_params=_cparams("parallel"),
        name="tail_final" if final else "tail",
    )(oas[0].reshape(t, A_WIDTH), oas[1], oas[2], sts[0].reshape(t, LANES), sts[1], sts[2],
      ob, proj, proj, x, expand, wa, wb, wo, g, w1, w2, fg)


def _prep_layer(l, norm_mix_g, w_in, wgf, bgf, wgb, bgb, gla_norm_g, w_branch_a, w_branch_b, w_out,
                norm_ffn_g, w_ff1, w_ff2):
    w = w_in[l]
    n_groups = len(A_GROUPS)
    a_end = 3 * n_groups * A_WIDTH
    gl1 = a_end + 2 * B_KW + 2 * B_VW + 2 * B_GATE_RANK
    col = jnp.arange(gl1)
    scale = jnp.where(col < n_groups * A_WIDTH, A_HEAD_DIM ** -0.5 * LOG2E,
                      jnp.where((col >= a_end) & (col < a_end + B_KW), B_DK ** -0.5, 1.0)).astype(w.dtype)
    w_fused = jnp.concatenate(
        [w[:, :gl1] * scale[None, :], jnp.zeros((D_MODEL, GL_PAD - 2 * B_GATE_RANK), w.dtype), w[:, gl1:]],
        axis=1)

    def gate_map(wg, row0):
        per_head = wg.reshape(B_GATE_RANK, B_HEADS, B_DK).transpose(1, 0, 2)
        return jnp.pad(per_head, ((0, 0), (row0, LANES - row0 - B_GATE_RANK), (0, 0))).astype(BF16)

    return dict(
        g_mix=norm_mix_g[l].reshape(1, D_MODEL),
        w_in=w_fused.astype(BF16),
        wgf=gate_map(wgf[l], 0),
        wgb=gate_map(wgb[l], B_GATE_RANK),
        bgf=bgf[l].reshape(B_HEADS, 1, B_DK),
        bgb=bgb[l].reshape(B_HEADS, 1, B_DK),
        gn=gla_norm_g[l].reshape(1, B_DV),
        wa=w_branch_a[l].astype(BF16),
        wb=w_branch_b[l].astype(BF16),
        wo=w_out[l].astype(BF16),
        g_ffn=norm_ffn_g[l].reshape(1, D_MODEL),
        w1=w_ff1[l].astype(BF16),
        w2=w_ff2[l].astype(BF16),
    )


def _trunk(x, layers, biases, expand, final_g):
    bsz, seq, _ = x.shape
    xf = x.reshape(bsz * seq, D_MODEL)
    for li, p in enumerate(layers):
        proj = _inproj(xf, p["g_mix"], p["w_in"])
        oas, sts = [], []
        for g, (_, dilation) in enumerate(A_GROUPS):
            o, st = _attention_group(proj, bsz, seq, g, dilation, biases[g])
            oas.append(o)
            sts.append(st)
        ob = _gla(proj, bsz, seq, p["wgf"], p["wgb"], p["bgf"], p["bgb"], p["gn"])
        xf = _tail(oas, sts, ob, proj, xf, expand, p["wa"], p["wb"], p["wo"],
                   p["g_ffn"], p["w1"], p["w2"], final_g, final=(li == len(layers) - 1))
    return xf.reshape(bsz, seq, D_MODEL)


def kernel(x_prompt, x_sample, norm_mix_g, w_in, gla_w_gate_fwd, gla_b_gate_fwd, gla_w_gate_bwd,
           gla_b_gate_bwd, gla_norm_g, w_branch_a, w_branch_b, w_out, norm_ffn_g, w_ff1, w_ff2,
           final_norm_g):
    layers = [_prep_layer(l, norm_mix_g, w_in, gla_w_gate_fwd, gla_b_gate_fwd, gla_w_gate_bwd,
                          gla_b_gate_bwd, gla_norm_g, w_branch_a, w_branch_b, w_out, norm_ffn_g,
                          w_ff1, w_ff2) for l in range(DEPTH)]
    n_heads = len(A_GROUPS) * A_HEADS
    slopes = jnp.exp2(-8.0 * jnp.arange(1, n_heads + 1, dtype=F32) / n_heads)
    biases = [_attn_bias(slopes[g * A_HEADS:(g + 1) * A_HEADS], dilation)
              for g, (_, dilation) in enumerate(A_GROUPS)]
    expand = (jnp.arange(LANES)[:, None] == jnp.arange(A_WIDTH)[None, :] // A_HEAD_DIM).astype(BF16)
    final_g = final_norm_g.reshape(1, D_MODEL)
    return (_trunk(x_prompt, layers, biases, expand, final_g),
            _trunk(x_sample, layers, biases, expand, final_g))
```

```python
import functools

import jax
import jax.numpy as jnp
from jax import lax
from jax.experimental import pallas as pl
from jax.experimental.pallas import tpu as pltpu

F32 = jnp.float32
BF16 = jnp.bfloat16

D_MODEL = 1024
DEPTH = 2
A_GROUPS = ((128, 1), (512, 4), (2048, 16))
A_HEADS = 8
A_HEAD_DIM = 64
A_WIDTH = A_HEADS * A_HEAD_DIM
A_QKV_W = 3 * A_WIDTH
A_HALF = 64
B_HEADS = 4
B_DK = 128
B_DV = 256
B_KW = B_HEADS * B_DK
B_VW = B_HEADS * B_DV
B_GATE_RANK = 16
B_GATE_TAU = 16.0
B_CHUNK = 64
D_FF = 4 * D_MODEL
EPS = 1e-6
NEG = -1e30
LOG2E = 1.4426950408889634
LANES = 128

COL_A = 0
COL_QB = COL_A + len(A_GROUPS) * A_QKV_W
COL_KB = COL_QB + B_KW
COL_VB = COL_KB + B_KW
COL_RB = COL_VB + B_VW
COL_GL = COL_RB + B_VW
GL_PAD = 512
COL_GA = COL_GL + GL_PAD
COL_GB = COL_GA + D_MODEL
PROJ_W = COL_GB + D_MODEL

VMEM_LIMIT_BYTES = 56 * 1024 * 1024

PERM_ROWS = 1024
IN_TM = PERM_ROWS
IN_TN = 2560
IN_SUB = 512
ROW_TM = 512
ATT_ROWS = 1024
ATT_SB = 128
ATT_KW = ATT_SB + 2 * A_HALF
GLA_ROWS = 2 * B_CHUNK
GLA_UNROLL = 8


def _cparams(*sem):
    return pltpu.CompilerParams(dimension_semantics=sem, vmem_limit_bytes=VMEM_LIMIT_BYTES)


def _resident(shape):
    nd = len(shape)
    return pl.BlockSpec(shape, lambda *_: (0,) * nd, pipeline_mode=pl.Buffered(1))


def _rms(x, g):
    return x * lax.rsqrt(jnp.mean(x * x, axis=-1, keepdims=True) + EPS) * g


def _sigmoid(t):
    return 1.0 / (1.0 + jnp.exp2(t * (-LOG2E)))


def _inproj_kernel(x_ref, g_ref, w_ref, o_ref, hf_ref, hs_ref):
    j = pl.program_id(1)

    @pl.when(j == 0)
    def _():
        h = _rms(x_ref[...], g_ref[...])
        hs_ref[0] = h.astype(BF16)
        d1, d2 = A_GROUPS[1][1], A_GROUPS[2][1]
        n1, n2, ratio = IN_TM // d1, IN_TM // d2, d2 // d1
        for s in range(D_MODEL // LANES):
            sl = slice(s * LANES, (s + 1) * LANES)
            hf_ref[0, s] = h[:, sl]
            for r in range(d1):
                part = hf_ref[0, s, pl.ds(r, n1, stride=d1), :]
                hf_ref[1, s, r * n1:(r + 1) * n1, :] = part
                hs_ref[1, r * n1:(r + 1) * n1, sl] = part.astype(BF16)
            for r in range(d2):
                r1, q = r % d1, r // d1
                hs_ref[2, r * n2:(r + 1) * n2, sl] = (
                    hf_ref[1, s, pl.ds(r1 * n1 + q, n2, stride=ratio), :].astype(BF16))

    n_groups = len(A_GROUPS)
    for c in range(IN_TN // IN_SUB):
        cc = j * (IN_TN // IN_SUB) + c
        k = jnp.where(cc < 3 * n_groups, cc % n_groups, 0)
        sl = slice(c * IN_SUB, (c + 1) * IN_SUB)
        o_ref[:, sl] = jnp.dot(hs_ref[k], w_ref[:, sl], preferred_element_type=F32).astype(BF16)


def _inproj(x, g, w):
    t = x.shape[0]
    return pl.pallas_call(
        _inproj_kernel,
        grid=(t // IN_TM, PROJ_W // IN_TN),
        in_specs=[
            pl.BlockSpec((IN_TM, D_MODEL), lambda i, j: (i, 0)),
            pl.BlockSpec((1, D_MODEL), lambda i, j: (0, 0)),
            pl.BlockSpec((D_MODEL, IN_TN), lambda i, j: (0, j)),
        ],
        out_specs=pl.BlockSpec((IN_TM, IN_TN), lambda i, j: (i, j)),
        out_shape=jax.ShapeDtypeStruct((t, PROJ_W), BF16),
        scratch_shapes=[pltpu.VMEM((2, D_MODEL // LANES, IN_TM, LANES), F32),
                        pltpu.VMEM((len(A_GROUPS), IN_TM, D_MODEL), BF16)],
        compiler_params=_cparams("parallel", "arbitrary"),
        name="inproj",
    )(x, g, w)


def _attn_kernel(q_ref, kp_ref, k_ref, kn_ref, vp_ref, v_ref, vn_ref, bm_ref, o_ref, st_ref,
                 *, tq, sub, run, n_res):
    i = pl.program_id(2)
    lane = lax.broadcasted_iota(jnp.int32, (ATT_SB, LANES), 1)
    low = lane < A_HEAD_DIM
    nt = (((1,), (1,)), ((), ()))

    def pieces(a, b):
        out, r = [], a
        while r < b:
            c = r // run
            e = min(b, (c + 1) * run)
            out.append((c, r - c * run, e - c * run))
            r = e
        return out

    def read(ref, rr, a, b, ls):
        parts = [ref[c, rr, s:e, ls] for c, s, e in pieces(a, b)]
        return parts[0] if len(parts) == 1 else jnp.concatenate(parts, axis=0)

    def window(prev_ref, ref, next_ref, rr, a, ls):
        parts = []
        if a < 0:
            parts.append(prev_ref[rr, :, ls])
        parts.append(read(ref, rr, max(a, 0), min(a + ATT_KW, tq), ls))
        if a + ATT_KW > tq:
            parts.append(next_ref[rr, :, ls])
        return jnp.concatenate(parts, axis=0)

    def write(ref, rr, a, ls, val):
        off = 0
        for c, s, e in pieces(a, a + ATT_SB):
            ref[c, rr, s:e, ls] = val[off:off + e - s]
            off += e - s

    for rr, jq in [(rr, jq) for rr in range(n_res) for jq in range(tq // ATT_SB)]:
        r0 = jq * ATT_SB
        kbase = i * tq + r0 - A_HALF
        variant = (kbase < 0).astype(jnp.int32) + 2 * (kbase + ATT_KW > sub).astype(jnp.int32)
        vws, scores = [], []
        for p in range(A_HEADS // 2):
            ls = slice(p * LANES, (p + 1) * LANES)
            qp = read(q_ref, rr, r0, r0 + ATT_SB, ls)
            kw = window(kp_ref, k_ref, kn_ref, rr, r0 - A_HALF, ls)
            vws.append(window(vp_ref, v_ref, vn_ref, rr, r0 - A_HALF, ls))
            zero = jnp.zeros_like(qp)
            scores.append(lax.dot_general(jnp.where(low, qp, zero), kw, nt, preferred_element_type=F32))
            scores.append(lax.dot_general(jnp.where(low, zero, qp), kw, nt, preferred_element_type=F32))
        probs = []
        st = jnp.ones((ATT_SB, LANES), F32)
        for h, s in enumerate(scores):
            t = s + bm_ref[variant * A_HEADS + h]
            m = jnp.max(t, axis=-1, keepdims=True)
            pe = jnp.exp2(t - m)
            probs.append(pe.astype(BF16))
            st = jnp.where(lane == h, m, st)
            st = jnp.where(lane == A_HEADS + h, jnp.sum(pe, axis=-1, keepdims=True), st)
        pvs = [jnp.dot(pe, vws[h // 2], preferred_element_type=F32) for h, pe in enumerate(probs)]
        for p in range(A_HEADS // 2):
            write(o_ref, rr, r0, slice(p * LANES, (p + 1) * LANES),
                  jnp.where(low, pvs[2 * p], pvs[2 * p + 1]).astype(BF16))
        write(st_ref, rr, r0, slice(0, LANES), st)


def _attn_bias(slopes, dilation):
    qi = jnp.arange(ATT_SB)[:, None]
    ci = jnp.arange(ATT_KW)[None, :]
    delta = ci - A_HALF - qi
    dist = (jnp.abs(delta) * dilation).astype(F32)
    bias = -slopes[:, None, None] * dist[None] * LOG2E
    band = jnp.abs(delta) <= A_HALF
    tables = []
    for v in range(4):
        ok = band
        if v & 1:
            ok = ok & (ci >= A_HALF)
        if v & 2:
            ok = ok & (ci < ATT_KW - A_HALF)
        tables.append(jnp.where(ok[None], bias, NEG))
    return jnp.concatenate(tables, axis=0)


def _attention_group(proj, bsz, seq, g, dilation, bm):
    sub = seq // dilation
    run = PERM_ROWS // dilation
    tq = min(ATT_ROWS, sub)
    assert tq % run == 0 and sub % tq == 0
    n_runs = tq // run
    n_res = min(dilation, ATT_ROWS // tq)
    n_tiles = seq // PERM_ROWS
    view = proj.reshape(bsz, n_tiles, dilation, run, PROJ_W)
    n_groups = len(A_GROUPS)
    cq = COL_A // A_WIDTH + g
    ck = cq + n_groups
    cv = ck + n_groups
    halo_per_run = run // A_HALF
    n_halo = sub // A_HALF
    halo_per_tq = tq // A_HALF

    def main(width, c):
        return pl.BlockSpec((None, n_runs, n_res, run, width), lambda b, r, i: (b, i, r, 0, c))

    def halo(c, which):
        def index(b, r, i):
            if which == "prev":
                hb = jnp.maximum(i * halo_per_tq - 1, 0)
            else:
                hb = jnp.minimum((i + 1) * halo_per_tq, n_halo - 1)
            return (b, hb // halo_per_run, r, hb % halo_per_run, c)
        return pl.BlockSpec((None, None, n_res, A_HALF, A_WIDTH), index)

    o, st = pl.pallas_call(
        functools.partial(_attn_kernel, tq=tq, sub=sub, run=run, n_res=n_res),
        grid=(bsz, dilation // n_res, sub // tq),
        in_specs=[main(A_WIDTH, cq), halo(ck, "prev"), main(A_WIDTH, ck), halo(ck, "next"),
                  halo(cv, "prev"), main(A_WIDTH, cv), halo(cv, "next"),
                  _resident((4 * A_HEADS, ATT_SB, ATT_KW))],
        out_specs=[main(A_WIDTH, 0), main(LANES, 0)],
        out_shape=[jax.ShapeDtypeStruct((bsz, n_tiles, dilation, run, A_WIDTH), BF16),
                   jax.ShapeDtypeStruct((bsz, n_tiles, dilation, run, LANES), F32)],
        compiler_params=_cparams("parallel", "parallel", "arbitrary"),
        name=f"attn_d{dilation}",
    )(view, view, view, view, view, view, view, bm)
    n_perm = bsz * n_tiles
    return (o.reshape(n_perm, dilation, run, A_WIDTH), st.reshape(n_perm, dilation, run, LANES))


def _gla_kernel(q_ref, k_ref, v_ref, r_ref, gl_ref, wgf_ref, wgb_ref, bf_ref, bb_ref, gn_ref,
                o_ref, oacc, sf, sb, *, seq):
    c = GLA_ROWS
    half = c // 2
    u = GLA_UNROLL
    n_blocks = seq // c
    sf[...] = jnp.zeros_like(sf)
    sb[...] = jnp.zeros_like(sb)
    ri = lax.broadcasted_iota(jnp.int32, (c, c), 0)
    ci = lax.broadcasted_iota(jnp.int32, (c, c), 1)
    tril = (ci <= ri).astype(BF16)
    tril2 = jnp.concatenate([tril, tril], axis=1)
    nt = (((1,), (1,)), ((), ()))

    def log2_decay(z):
        e = jnp.exp2(jnp.abs(z) * (-LOG2E))
        return (jnp.minimum(z, 0.0) - jnp.log(1.0 + e)) * (LOG2E / B_GATE_TAU)

    def prefix_sum(la):
        hi = la.astype(BF16)
        lo = (la - hi.astype(F32)).astype(BF16)
        return jnp.dot(tril2, jnp.concatenate([hi, lo], axis=0), preferred_element_type=F32)

    def exponents(la, cs, forward):
        tot = cs[c - 1:c, :]
        if forward:
            ref = cs[half - 1:half, :]
            return cs - ref, ref, tot
        suf = tot - cs + la
        ref = suf[half:half + 1, :]
        return suf - ref, ref, tot

    def body(nb, finish):
        rows_f = [pl.multiple_of((nb * u + j) * c, c) for j in range(u)]
        rows_b = [pl.multiple_of((n_blocks - 1 - nb * u - j) * c, c) for j in range(u)]
        chains = [(r, True) for r in rows_f] + [(r, False) for r in rows_b]
        zs = [jnp.dot(gl_ref[0, pl.ds(r, c), :], (wgf_ref if fw else wgb_ref)[0],
                      preferred_element_type=F32) + (bf_ref if fw else bb_ref)[0]
              for r, fw in chains]
        las = [log2_decay(z) for z in zs]
        css = [prefix_sum(la) for la in las]
        qes, kes, qis, kds, decs = [], [], [], [], []
        for (r, fw), la, cs in zip(chains, las, css):
            rel, ref, tot = exponents(la, cs, fw)
            qp = q_ref[0, pl.ds(r, c), :].astype(F32) * jnp.exp2(rel)
            kp = k_ref[0, pl.ds(r, c), :].astype(F32) * jnp.exp2(-rel)
            qes.append(qp.astype(BF16))
            kes.append(kp.astype(BF16))
            qis.append((qp * jnp.exp2(ref)).astype(BF16))
            kds.append(kp * jnp.exp2(tot - ref))
            decs.append(jnp.exp2(tot))
        atts = [lax.dot_general(qe, ke, nt, preferred_element_type=F32) for qe, ke in zip(qes, kes)]
        atts = [jnp.where((ci <= ri) if fw else (ci > ri), a, 0.0).astype(BF16)
                for (_, fw), a in zip(chains, atts)]
        vs = [v_ref[0, pl.ds(r, c), :] for r, _ in chains]
        ups = [jnp.dot(kd.T.astype(BF16), v, preferred_element_type=F32) for kd, v in zip(kds, vs)]
        states = []
        for st_ref, first in ((sf, 0), (sb, u)):
            st = st_ref[...]
            for idx in range(first, first + u):
                states.append(st.astype(BF16))
                dcol = jnp.broadcast_to(decs[idx], (B_DK, B_DK)).T
                st = st * jnp.concatenate([dcol, dcol], axis=1) + ups[idx]
            st_ref[...] = st
        outs = [jnp.dot(jnp.concatenate([qi, a], axis=1), jnp.concatenate([st, v], axis=0),
                        preferred_element_type=F32)
                for a, v, qi, st in zip(atts, vs, qis, states)]
        for (r, _), o in zip(chains, outs):
            if finish:
                y = _rms(o + oacc[pl.ds(r, c), :], gn_ref[...])
                rr = r_ref[0, pl.ds(r, c), :].astype(F32)
                o_ref[0, pl.ds(r, c), :] = (y * (rr * _sigmoid(rr))).astype(BF16)
            else:
                oacc[pl.ds(r, c), :] = o

    def first_half(nb, carry):
        body(nb, False)
        return carry

    def second_half(nb, carry):
        body(nb, True)
        return carry

    n_bodies = n_blocks // u
    lax.fori_loop(0, n_bodies // 2, first_half, 0)
    lax.fori_loop(n_bodies // 2, n_bodies, second_half, 0)


def _gla(proj, bsz, seq, wgf, wgb, bgf, bgb, gn):
    view = proj.reshape(bsz, seq, PROJ_W)

    def cols(width, col0):
        return pl.BlockSpec((1, seq, width), lambda b, h: (b, 0, col0 // width + h))

    def per_head(shape):
        return pl.BlockSpec((1,) + shape, lambda b, h: (h, 0, 0))

    out = pl.pallas_call(
        functools.partial(_gla_kernel, seq=seq),
        grid=(bsz, B_HEADS),
        in_specs=[cols(B_DK, COL_QB), cols(B_DK, COL_KB), cols(B_DV, COL_VB), cols(B_DV, COL_RB),
                  pl.BlockSpec((1, seq, LANES), lambda b, h: (b, 0, COL_GL // LANES)),
                  per_head((LANES, B_DK)), per_head((LANES, B_DK)),
                  per_head((1, B_DK)), per_head((1, B_DK)),
                  pl.BlockSpec((1, B_DV), lambda b, h: (0, 0))],
        out_specs=pl.BlockSpec((1, seq, B_DV), lambda b, h: (b, 0, h)),
        out_shape=jax.ShapeDtypeStruct((bsz, seq, B_VW), BF16),
        scratch_shapes=[pltpu.VMEM((seq, B_DV), F32),
                        pltpu.VMEM((B_DK, B_DV), F32),
                        pltpu.VMEM((B_DK, B_DV), F32)],
        compiler_params=_cparams("parallel", "arbitrary"),
        name="gla",
    )(view, view, view, view, view, wgf, wgb, bgf, bgb, gn)
    return out.reshape(bsz * seq, B_VW)


def _merge_rows(o1_ref, o2_ref, o3_ref, s1_ref, s2_ref, s3_ref, ob_ref, ga_ref, gb_ref, x_ref,
                e_ref, wa_ref, wb_ref, wo_ref, on2, on3, sn2, sn3):
    n_slabs = A_WIDTH // LANES
    for (_, d), o_ref, s_ref, on, sn in ((A_GROUPS[1], o2_ref, s2_ref, on2, sn2),
                                        (A_GROUPS[2], o3_ref, s3_ref, on3, sn3)):
        n = ROW_TM // d
        for r in range(d):
            sn[pl.ds(r, n, stride=d), :] = s_ref[r]
            for s in range(n_slabs):
                on[s, pl.ds(r, n, stride=d), :] = o_ref[r, :, s * LANES:(s + 1) * LANES].astype(F32)

    ms = (s1_ref[...], sn2[...], sn3[...])
    ls = [pltpu.roll(m, LANES - A_HEADS, axis=1) for m in ms]
    top = jnp.maximum(jnp.maximum(ms[0], ms[1]), ms[2])
    es = [jnp.exp2(m - top) for m in ms]
    inv = 1.0 / (es[0] * ls[0] + es[1] * ls[1] + es[2] * ls[2])
    ws = [jnp.dot((e * inv).astype(BF16), e_ref[...], preferred_element_type=F32) for e in es]
    slabs = []
    for s in range(n_slabs):
        sl = slice(s * LANES, (s + 1) * LANES)
        slabs.append(o1_ref[:, sl].astype(F32) * ws[0][:, sl]
                     + on2[s] * ws[1][:, sl] + on3[s] * ws[2][:, sl])
    oa = jnp.concatenate(slabs, axis=1)
    ya = jnp.dot(oa.astype(BF16), wa_ref[...], preferred_element_type=F32)
    yb = jnp.dot(ob_ref[...], wb_ref[...], preferred_element_type=F32)
    merged = _sigmoid(ga_ref[...].astype(F32)) * ya + _sigmoid(gb_ref[...].astype(F32)) * yb
    return x_ref[...] + jnp.dot(merged.astype(BF16), wo_ref[...], preferred_element_type=F32)


def _mlp_rows(x, g_ref, w1_ref, w2_ref):
    h = (x * g_ref[...]).astype(BF16)
    acc = jnp.zeros_like(x)
    for c in range(D_FF // D_MODEL):
        sl = slice(c * D_MODEL, (c + 1) * D_MODEL)
        u = jnp.maximum(jnp.dot(h, w1_ref[:, sl], preferred_element_type=F32), 0.0)
        acc = acc + jnp.dot((u * u).astype(BF16), w2_ref[sl, :], preferred_element_type=F32)
    return x + acc * (1.0 / (jnp.mean(x * x, axis=-1, keepdims=True) + EPS))


def _tail_kernel(o1_ref, o2_ref, o3_ref, s1_ref, s2_ref, s3_ref, ob_ref, ga_ref, gb_ref, x_ref,
                 e_ref, wa_ref, wb_ref, wo_ref, g_ref, w1_ref, w2_ref, fg_ref, out_ref,
                 on2, on3, sn2, sn3, *, final):
    x1 = _merge_rows(o1_ref, o2_ref, o3_ref, s1_ref, s2_ref, s3_ref, ob_ref, ga_ref, gb_ref, x_ref,
                     e_ref, wa_ref, wb_ref, wo_ref, on2, on3, sn2, sn3)
    x2 = _mlp_rows(x1, g_ref, w1_ref, w2_ref)
    out_ref[...] = _rms(x2, fg_ref[...]) if final else x2


def _tail(oas, sts, ob, proj, x, expand, wa, wb, wo, g, w1, w2, fg, final):
    t = x.shape[0]
    per_perm = PERM_ROWS // ROW_TM

    def rows(width, cblk=0):
        return pl.BlockSpec((ROW_TM, width), lambda i: (i, cblk))

    def perm(width, d):
        return pl.BlockSpec((None, d, ROW_TM // d, width), lambda i: (i // per_perm, 0, i % per_perm, 0))

    d2, d3 = A_GROUPS[1][1], A_GROUPS[2][1]
    return pl.pallas_call(
        functools.partial(_tail_kernel, final=final),
        grid=(t // ROW_TM,),
        in_specs=[rows(A_WIDTH), perm(A_WIDTH, d2), perm(A_WIDTH, d3),
                  rows(LANES), perm(LANES, d2), perm(LANES, d3),
                  rows(B_VW), rows(D_MODEL, COL_GA // D_MODEL), rows(D_MODEL, COL_GB // D_MODEL),
                  rows(D_MODEL),
                  _resident((LANES, A_WIDTH)), _resident((A_WIDTH, D_MODEL)),
                  _resident((B_VW, D_MODEL)), _resident((D_MODEL, D_MODEL)),
                  _resident((1, D_MODEL)), _resident((D_MODEL, D_FF)), _resident((D_FF, D_MODEL)),
                  _resident((1, D_MODEL))],
        out_specs=rows(D_MODEL),
        out_shape=jax.ShapeDtypeStruct((t, D_MODEL), F32),
        scratch_shapes=[pltpu.VMEM((A_WIDTH // LANES, ROW_TM, LANES), F32),
                        pltpu.VMEM((A_WIDTH // LANES, ROW_TM, LANES), F32),
                        pltpu.VMEM((ROW_TM, LANES), F32),
                        pltpu.VMEM((ROW_TM, LANES), F32)],
        compiler_params=_cparams("parallel"),
        name="tail_final" if final else "tail",
    )(oas[0].reshape(t, A_WIDTH), oas[1], oas[2], sts[0].reshape(t, LANES), sts[1], sts[2],
      ob, proj, proj, x, expand, wa, wb, wo, g, w1, w2, fg)


def _prep_layer(l, norm_mix_g, w_in, wgf, bgf, wgb, bgb, gla_norm_g, w_branch_a, w_branch_b, w_out,
                norm_ffn_g, w_ff1, w_ff2):
    w = w_in[l]
    n_groups = len(A_GROUPS)
    a_end = 3 * n_groups * A_WIDTH
    gl1 = a_end + 2 * B_KW + 2 * B_VW + 2 * B_GATE_RANK
    col = jnp.arange(gl1)
    scale = jnp.where(col < n_groups * A_WIDTH, A_HEAD_DIM ** -0.5 * LOG2E,
                      jnp.where((col >= a_end) & (col < a_end + B_KW), B_DK ** -0.5, 1.0)).astype(w.dtype)
    w_fused = jnp.concatenate(
        [w[:, :gl1] * scale[None, :], jnp.zeros((D_MODEL, GL_PAD - 2 * B_GATE_RANK), w.dtype), w[:, gl1:]],
        axis=1)

    def gate_map(wg, row0):
        per_head = wg.reshape(B_GATE_RANK, B_HEADS, B_DK).transpose(1, 0, 2)
        return jnp.pad(per_head, ((0, 0), (row0, LANES - row0 - B_GATE_RANK), (0, 0))).astype(BF16)

    return dict(
        g_mix=norm_mix_g[l].reshape(1, D_MODEL),
        w_in=w_fused.astype(BF16),
        wgf=gate_map(wgf[l], 0),
        wgb=gate_map(wgb[l], B_GATE_RANK),
        bgf=bgf[l].reshape(B_HEADS, 1, B_DK),
        bgb=bgb[l].reshape(B_HEADS, 1, B_DK),
        gn=gla_norm_g[l].reshape(1, B_DV),
        wa=w_branch_a[l].astype(BF16),
        wb=w_branch_b[l].astype(BF16),
        wo=w_out[l].astype(BF16),
        g_ffn=norm_ffn_g[l].reshape(1, D_MODEL),
        w1=w_ff1[l].astype(BF16),
        w2=w_ff2[l].astype(BF16),
    )


def _trunk(x, layers, biases, expand, final_g):
    bsz, seq, _ = x.shape
    xf = x.reshape(bsz * seq, D_MODEL)
    for li, p in enumerate(layers):
        proj = _inproj(xf, p["g_mix"], p["w_in"])
        oas, sts = [], []
        for g, (_, dilation) in enumerate(A_GROUPS):
            o, st = _attention_group(proj, bsz, seq, g, dilation, biases[g])
            oas.append(o)
            sts.append(st)
        ob = _gla(proj, bsz, seq, p["wgf"], p["wgb"], p["bgf"], p["bgb"], p["gn"])
        xf = _tail(oas, sts, ob, proj, xf, expand, p["wa"], p["wb"], p["wo"],
                   p["g_ffn"], p["w1"], p["w2"], final_g, final=(li == len(layers) - 1))
    return xf.reshape(bsz, seq, D_MODEL)


def kernel(x_prompt, x_sample, norm_mix_g, w_in, gla_w_gate_fwd, gla_b_gate_fwd, gla_w_gate_bwd,
           gla_b_gate_bwd, gla_norm_g, w_branch_a, w_branch_b, w_out, norm_ffn_g, w_ff1, w_ff2,
           final_norm_g):
    layers = [_prep_layer(l, norm_mix_g, w_in, gla_w_gate_fwd, gla_b_gate_fwd, gla_w_gate_bwd,
                          gla_b_gate_bwd, gla_norm_g, w_branch_a, w_branch_b, w_out, norm_ffn_g,
                          w_ff1, w_ff2) for l in range(DEPTH)]
    n_heads = len(A_GROUPS) * A_HEADS
    slopes = jnp.exp2(-8.0 * jnp.arange(1, n_heads + 1, dtype=F32) / n_heads)
    biases = [_attn_bias(slopes[g * A_HEADS:(g + 1) * A_HEADS], dilation)
              for g, (_, dilation) in enumerate(A_GROUPS)]
    expand = (jnp.arange(LANES)[:, None] == jnp.arange(A_WIDTH)[None, :] // A_HEAD_DIM).astype(BF16)
    final_g = final_norm_g.reshape(1, D_MODEL)
    return (_trunk(x_prompt, layers, biases, expand, final_g),
            _trunk(x_sample, layers, biases, expand, final_g))
```

```python
import functools

import jax
import jax.numpy as jnp
from jax import lax
from jax.experimental import pallas as pl
from jax.experimental.pallas import tpu as pltpu

F32 = jnp.float32
BF16 = jnp.bfloat16

D_MODEL = 1024
DEPTH = 2
A_GROUPS = ((128, 1), (512, 4), (2048, 16))
A_HEADS = 8
A_HEAD_DIM = 64
A_WIDTH = A_HEADS * A_HEAD_DIM
A_QKV_W = 3 * A_WIDTH
A_HALF = 64
B_HEADS = 4
B_DK = 128
B_DV = 256
B_KW = B_HEADS * B_DK
B_VW = B_HEADS * B_DV
B_GATE_RANK = 16
B_GATE_TAU = 16.0
B_CHUNK = 64
D_FF = 4 * D_MODEL
EPS = 1e-6
NEG = -1e30
LOG2E = 1.4426950408889634
LANES = 128

COL_A = 0
COL_QB = COL_A + len(A_GROUPS) * A_QKV_W
COL_KB = COL_QB + B_KW
COL_VB = COL_KB + B_KW
COL_RB = COL_VB + B_VW
COL_GL = COL_RB + B_VW
GL_PAD = 512
COL_GA = COL_GL + GL_PAD
COL_GB = COL_GA + D_MODEL
PROJ_W = COL_GB + D_MODEL

VMEM_LIMIT_BYTES = 56 * 1024 * 1024

PERM_ROWS = 1024
IN_TM = PERM_ROWS
IN_TN = 2560
IN_SUB = 512
ROW_TM = 512
ATT_ROWS = 2048
ATT_SB = 128
ATT_KW = ATT_SB + 2 * A_HALF
GLA_ROWS = 2 * B_CHUNK
GLA_UNROLL = 16


def _cparams(*sem):
    return pltpu.CompilerParams(dimension_semantics=sem, vmem_limit_bytes=VMEM_LIMIT_BYTES)


def _resident(shape):
    nd = len(shape)
    return pl.BlockSpec(shape, lambda *_: (0,) * nd, pipeline_mode=pl.Buffered(1))


def _rms(x, g):
    return x * lax.rsqrt(jnp.mean(x * x, axis=-1, keepdims=True) + EPS) * g


def _sigmoid(t):
    return 1.0 / (1.0 + jnp.exp2(t * (-LOG2E)))


def _inproj_kernel(x_ref, g_ref, w_ref, o_ref, hf_ref, hs_ref):
    j = pl.program_id(1)

    @pl.when(j == 0)
    def _():
        h = _rms(x_ref[...], g_ref[...])
        hs_ref[0] = h.astype(BF16)
        d1, d2 = A_GROUPS[1][1], A_GROUPS[2][1]
        n1, n2, ratio = IN_TM // d1, IN_TM // d2, d2 // d1
        for s in range(D_MODEL // LANES):
            sl = slice(s * LANES, (s + 1) * LANES)
            hf_ref[0, s] = h[:, sl]
            for r in range(d1):
                part = hf_ref[0, s, pl.ds(r, n1, stride=d1), :]
                hf_ref[1, s, r * n1:(r + 1) * n1, :] = part
                hs_ref[1, r * n1:(r + 1) * n1, sl] = part.astype(BF16)
            for r in range(d2):
                r1, q = r % d1, r // d1
                hs_ref[2, r * n2:(r + 1) * n2, sl] = (
                    hf_ref[1, s, pl.ds(r1 * n1 + q, n2, stride=ratio), :].astype(BF16))

    n_groups = len(A_GROUPS)
    for c in range(IN_TN // IN_SUB):
        cc = j * (IN_TN // IN_SUB) + c
        k = jnp.where(cc < 3 * n_groups, cc % n_groups, 0)
        sl = slice(c * IN_SUB, (c + 1) * IN_SUB)
        o_ref[:, sl] = jnp.dot(hs_ref[k], w_ref[:, sl], preferred_element_type=F32).astype(BF16)


def _inproj(x, g, w):
    t = x.shape[0]
    return pl.pallas_call(
        _inproj_kernel,
        grid=(t // IN_TM, PROJ_W // IN_TN),
        in_specs=[
            pl.BlockSpec((IN_TM, D_MODEL), lambda i, j: (i, 0)),
            pl.BlockSpec((1, D_MODEL), lambda i, j: (0, 0)),
            pl.BlockSpec((D_MODEL, IN_TN), lambda i, j: (0, j)),
        ],
        out_specs=pl.BlockSpec((IN_TM, IN_TN), lambda i, j: (i, j)),
        out_shape=jax.ShapeDtypeStruct((t, PROJ_W), BF16),
        scratch_shapes=[pltpu.VMEM((2, D_MODEL // LANES, IN_TM, LANES), F32),
                        pltpu.VMEM((len(A_GROUPS), IN_TM, D_MODEL), BF16)],
        compiler_params=_cparams("parallel", "arbitrary"),
        name="inproj",
    )(x, g, w)


def _attn_kernel(q_ref, kp_ref, k_ref, kn_ref, vp_ref, v_ref, vn_ref, bm_ref, o_ref, st_ref,
                 *, tq, sub, run, n_res):
    i = pl.program_id(2)
    lane = lax.broadcasted_iota(jnp.int32, (ATT_SB, LANES), 1)
    low = lane < A_HEAD_DIM
    nt = (((1,), (1,)), ((), ()))

    def pieces(a, b):
        out, r = [], a
        while r < b:
            c = r // run
            e = min(b, (c + 1) * run)
            out.append((c, r - c * run, e - c * run))
            r = e
        return out

    def read(ref, rr, a, b, ls):
        parts = [ref[c, rr, s:e, ls] for c, s, e in pieces(a, b)]
        return parts[0] if len(parts) == 1 else jnp.concatenate(parts, axis=0)

    def window(prev_ref, ref, next_ref, rr, a, ls):
        parts = []
        if a < 0:
            parts.append(prev_ref[rr, :, ls])
        parts.append(read(ref, rr, max(a, 0), min(a + ATT_KW, tq), ls))
        if a + ATT_KW > tq:
            parts.append(next_ref[rr, :, ls])
        return jnp.concatenate(parts, axis=0)

    def write(ref, rr, a, ls, val):
        off = 0
        for c, s, e in pieces(a, a + ATT_SB):
            ref[c, rr, s:e, ls] = val[off:off + e - s]
            off += e - s

    for rr, jq in [(rr, jq) for rr in range(n_res) for jq in range(tq // ATT_SB)]:
        r0 = jq * ATT_SB
        kbase = i * tq + r0 - A_HALF
        variant = (kbase < 0).astype(jnp.int32) + 2 * (kbase + ATT_KW > sub).astype(jnp.int32)
        vws, scores = [], []
        for p in range(A_HEADS // 2):
            ls = slice(p * LANES, (p + 1) * LANES)
            qp = read(q_ref, rr, r0, r0 + ATT_SB, ls)
            kw = window(kp_ref, k_ref, kn_ref, rr, r0 - A_HALF, ls)
            vws.append(window(vp_ref, v_ref, vn_ref, rr, r0 - A_HALF, ls))
            zero = jnp.zeros_like(qp)
            scores.append(lax.dot_general(jnp.where(low, qp, zero), kw, nt, preferred_element_type=F32))
            scores.append(lax.dot_general(jnp.where(low, zero, qp), kw, nt, preferred_element_type=F32))
        probs = []
        st = jnp.ones((ATT_SB, LANES), F32)
        for h, s in enumerate(scores):
            t = s + bm_ref[variant * A_HEADS + h]
            m = jnp.max(t, axis=-1, keepdims=True)
            pe = jnp.exp2(t - m)
            probs.append(pe.astype(BF16))
            st = jnp.where(lane == h, m, st)
            st = jnp.where(lane == A_HEADS + h, jnp.sum(pe, axis=-1, keepdims=True), st)
        pvs = [jnp.dot(pe, vws[h // 2], preferred_element_type=F32) for h, pe in enumerate(probs)]
        for p in range(A_HEADS // 2):
            write(o_ref, rr, r0, slice(p * LANES, (p + 1) * LANES),
                  jnp.where(low, pvs[2 * p], pvs[2 * p + 1]).astype(BF16))
        write(st_ref, rr, r0, slice(0, LANES), st)


def _attn_bias(slopes, dilation):
    qi = jnp.arange(ATT_SB)[:, None]
    ci = jnp.arange(ATT_KW)[None, :]
    delta = ci - A_HALF - qi
    dist = (jnp.abs(delta) * dilation).astype(F32)
    bias = -slopes[:, None, None] * dist[None] * LOG2E
    band = jnp.abs(delta) <= A_HALF
    tables = []
    for v in range(4):
        ok = band
        if v & 1:
            ok = ok & (ci >= A_HALF)
        if v & 2:
            ok = ok & (ci < ATT_KW - A_HALF)
        tables.append(jnp.where(ok[None], bias, NEG))
    return jnp.concatenate(tables, axis=0)


def _attention_group(proj, bsz, seq, g, dilation, bm):
    sub = seq // dilation
    run = PERM_ROWS // dilation
    tq = min(ATT_ROWS, sub)
    assert tq % run == 0 and sub % tq == 0
    n_runs = tq // run
    n_res = min(dilation, ATT_ROWS // tq)
    n_tiles = seq // PERM_ROWS
    view = proj.reshape(bsz, n_tiles, dilation, run, PROJ_W)
    n_groups = len(A_GROUPS)
    cq = COL_A // A_WIDTH + g
    ck = cq + n_groups
    cv = ck + n_groups
    halo_per_run = run // A_HALF
    n_halo = sub // A_HALF
    halo_per_tq = tq // A_HALF

    def main(width, c):
        return pl.BlockSpec((None, n_runs, n_res, run, width), lambda b, r, i: (b, i, r, 0, c))

    def halo(c, which):
        def index(b, r, i):
            if which == "prev":
                hb = jnp.maximum(i * halo_per_tq - 1, 0)
            else:
                hb = jnp.minimum((i + 1) * halo_per_tq, n_halo - 1)
            return (b, hb // halo_per_run, r, hb % halo_per_run, c)
        return pl.BlockSpec((None, None, n_res, A_HALF, A_WIDTH), index)

    o, st = pl.pallas_call(
        functools.partial(_attn_kernel, tq=tq, sub=sub, run=run, n_res=n_res),
        grid=(bsz, dilation // n_res, sub // tq),
        in_specs=[main(A_WIDTH, cq), halo(ck, "prev"), main(A_WIDTH, ck), halo(ck, "next"),
                  halo(cv, "prev"), main(A_WIDTH, cv), halo(cv, "next"),
                  _resident((4 * A_HEADS, ATT_SB, ATT_KW))],
        out_specs=[main(A_WIDTH, 0), main(LANES, 0)],
        out_shape=[jax.ShapeDtypeStruct((bsz, n_tiles, dilation, run, A_WIDTH), BF16),
                   jax.ShapeDtypeStruct((bsz, n_tiles, dilation, run, LANES), F32)],
        compiler_params=_cparams("parallel", "parallel", "arbitrary"),
        name=f"attn_d{dilation}",
    )(view, view, view, view, view, view, view, bm)
    n_perm = bsz * n_tiles
    return (o.reshape(n_perm, dilation, run, A_WIDTH), st.reshape(n_perm, dilation, run, LANES))


def _gla_kernel(q_ref, k_ref, v_ref, r_ref, gl_ref, wgf_ref, wgb_ref, bf_ref, bb_ref, gn_ref,
                o_ref, oacc, sf, sb, *, seq):
    c = GLA_ROWS
    half = c // 2
    n_blocks = seq // c
    u = min(GLA_UNROLL, n_blocks // 2)
    sf[...] = jnp.zeros_like(sf)
    sb[...] = jnp.zeros_like(sb)
    ri = lax.broadcasted_iota(jnp.int32, (c, c), 0)
    ci = lax.broadcasted_iota(jnp.int32, (c, c), 1)
    tril = (ci <= ri).astype(BF16)
    tril2 = jnp.concatenate([tril, tril], axis=1)
    nt = (((1,), (1,)), ((), ()))

    def log2_decay(z):
        e = jnp.exp2(jnp.abs(z) * (-LOG2E))
        return (jnp.minimum(z, 0.0) - jnp.log(1.0 + e)) * (LOG2E / B_GATE_TAU)

    def prefix_sum(la):
        hi = la.astype(BF16)
        lo = (la - hi.astype(F32)).astype(BF16)
        return jnp.dot(tril2, jnp.concatenate([hi, lo], axis=0), preferred_element_type=F32)

    def exponents(la, cs, forward):
        tot = cs[c - 1:c, :]
        if forward:
            ref = cs[half - 1:half, :]
            return cs - ref, ref, tot
        suf = tot - cs + la
        ref = suf[half:half + 1, :]
        return suf - ref, ref, tot

    def body(nb, finish):
        rows_f = [pl.multiple_of((nb * u + j) * c, c) for j in range(u)]
        rows_b = [pl.multiple_of((n_blocks - 1 - nb * u - j) * c, c) for j in range(u)]
        chains = [(r, True) for r in rows_f] + [(r, False) for r in rows_b]
        zs = [jnp.dot(gl_ref[0, pl.ds(r, c), :], (wgf_ref if fw else wgb_ref)[0],
                      preferred_element_type=F32) + (bf_ref if fw else bb_ref)[0]
              for r, fw in chains]
        las = [log2_decay(z) for z in zs]
        css = [prefix_sum(la) for la in las]
        qes, kes, qis, kds, decs = [], [], [], [], []
        for (r, fw), la, cs in zip(chains, las, css):
            rel, ref, tot = exponents(la, cs, fw)
            qp = q_ref[0, pl.ds(r, c), :].astype(F32) * jnp.exp2(rel)
            kp = k_ref[0, pl.ds(r, c), :].astype(F32) * jnp.exp2(-rel)
            qes.append(qp.astype(BF16))
            kes.append(kp.astype(BF16))
            qis.append((qp * jnp.exp2(ref)).astype(BF16))
            kds.append(kp * jnp.exp2(tot - ref))
            decs.append(jnp.exp2(tot))
        atts = [lax.dot_general(qe, ke, nt, preferred_element_type=F32) for qe, ke in zip(qes, kes)]
        atts = [jnp.where((ci <= ri) if fw else (ci > ri), a, 0.0).astype(BF16)
                for (_, fw), a in zip(chains, atts)]
        vs = [v_ref[0, pl.ds(r, c), :] for r, _ in chains]
        ups = [jnp.dot(kd.T.astype(BF16), v, preferred_element_type=F32) for kd, v in zip(kds, vs)]
        states = []
        for st_ref, first in ((sf, 0), (sb, u)):
            st = st_ref[...]
            for idx in range(first, first + u):
                states.append(st.astype(BF16))
                dcol = jnp.broadcast_to(decs[idx], (B_DK, B_DK)).T
                st = st * jnp.concatenate([dcol, dcol], axis=1) + ups[idx]
            st_ref[...] = st
        outs = [jnp.dot(jnp.concatenate([qi, a], axis=1), jnp.concatenate([st, v], axis=0),
                        preferred_element_type=F32)
                for a, v, qi, st in zip(atts, vs, qis, states)]
        for (r, _), o in zip(chains, outs):
            if finish:
                y = _rms(o + oacc[pl.ds(r, c), :], gn_ref[...])
                rr = r_ref[0, pl.ds(r, c), :].astype(F32)
                o_ref[0, pl.ds(r, c), :] = (y * (rr * _sigmoid(rr))).astype(BF16)
            else:
                oacc[pl.ds(r, c), :] = o

    def first_half(nb, carry):
        body(nb, False)
        return carry

    def second_half(nb, carry):
        body(nb, True)
        return carry

    n_bodies = n_blocks // u
    lax.fori_loop(0, n_bodies // 2, first_half, 0)
    lax.fori_loop(n_bodies // 2, n_bodies, second_half, 0)


def _gla(proj, bsz, seq, wgf, wgb, bgf, bgb, gn):
    view = proj.reshape(bsz, seq, PROJ_W)

    def cols(width, col0):
        return pl.BlockSpec((1, seq, width), lambda b, h: (b, 0, col0 // width + h))

    def per_head(shape):
        return pl.BlockSpec((1,) + shape, lambda b, h: (h, 0, 0))

    out = pl.pallas_call(
        functools.partial(_gla_kernel, seq=seq),
        grid=(bsz, B_HEADS),
        in_specs=[cols(B_DK, COL_QB), cols(B_DK, COL_KB), cols(B_DV, COL_VB), cols(B_DV, COL_RB),
                  pl.BlockSpec((1, seq, LANES), lambda b, h: (b, 0, COL_GL // LANES)),
                  per_head((LANES, B_DK)), per_head((LANES, B_DK)),
                  per_head((1, B_DK)), per_head((1, B_DK)),
                  pl.BlockSpec((1, B_DV), lambda b, h: (0, 0))],
        out_specs=pl.BlockSpec((1, seq, B_DV), lambda b, h: (b, 0, h)),
        out_shape=jax.ShapeDtypeStruct((bsz, seq, B_VW), BF16),
        scratch_shapes=[pltpu.VMEM((seq, B_DV), F32),
                        pltpu.VMEM((B_DK, B_DV), F32),
                        pltpu.VMEM((B_DK, B_DV), F32)],
        compiler_params=_cparams("parallel", "arbitrary"),
        name="gla",
    )(view, view, view, view, view, wgf, wgb, bgf, bgb, gn)
    return out.reshape(bsz * seq, B_VW)


def _merge_rows(o1_ref, o2_ref, o3_ref, s1_ref, s2_ref, s3_ref, ob_ref, ga_ref, gb_ref, x_ref,
                e_ref, wa_ref, wb_ref, wo_ref, on2, on3, sn2, sn3):
    n_slabs = A_WIDTH // LANES
    for (_, d), o_ref, s_ref, on, sn in ((A_GROUPS[1], o2_ref, s2_ref, on2, sn2),
                                        (A_GROUPS[2], o3_ref, s3_ref, on3, sn3)):
        n = ROW_TM // d
        for r in range(d):
            sn[pl.ds(r, n, stride=d), :] = s_ref[r]
            for s in range(n_slabs):
                on[s, pl.ds(r, n, stride=d), :] = o_ref[r, :, s * LANES:(s + 1) * LANES].astype(F32)

    ms = (s1_ref[...], sn2[...], sn3[...])
    ls = [pltpu.roll(m, LANES - A_HEADS, axis=1) for m in ms]
    top = jnp.maximum(jnp.maximum(ms[0], ms[1]), ms[2])
    es = [jnp.exp2(m - top) for m in ms]
    inv = 1.0 / (es[0] * ls[0] + es[1] * ls[1] + es[2] * ls[2])
    ws = [jnp.dot((e * inv).astype(BF16), e_ref[...], preferred_element_type=F32) for e in es]
    slabs = []
    for s in range(n_slabs):
        sl = slice(s * LANES, (s + 1) * LANES)
        slabs.append(o1_ref[:, sl].astype(F32) * ws[0][:, sl]
                     + on2[s] * ws[1][:, sl] + on3[s] * ws[2][:, sl])
    oa = jnp.concatenate(slabs, axis=1)
    ya = jnp.dot(oa.astype(BF16), wa_ref[...], preferred_element_type=F32)
    yb = jnp.dot(ob_ref[...], wb_ref[...], preferred_element_type=F32)
    merged = _sigmoid(ga_ref[...].astype(F32)) * ya + _sigmoid(gb_ref[...].astype(F32)) * yb
    return x_ref[...] + jnp.dot(merged.astype(BF16), wo_ref[...], preferred_element_type=F32)


def _mlp_rows(x, g_ref, w1_ref, w2_ref):
    h = (x * g_ref[...]).astype(BF16)
    acc = jnp.zeros_like(x)
    for c in range(D_FF // D_MODEL):
        sl = slice(c * D_MODEL, (c + 1) * D_MODEL)
        u = jnp.maximum(jnp.dot(h, w1_ref[:, sl], preferred_element_type=F32), 0.0)
        acc = acc + jnp.dot((u * u).astype(BF16), w2_ref[sl, :], preferred_element_type=F32)
    return x + acc * (1.0 / (jnp.mean(x * x, axis=-1, keepdims=True) + EPS))


def _tail_kernel(o1_ref, o2_ref, o3_ref, s1_ref, s2_ref, s3_ref, ob_ref, ga_ref, gb_ref, x_ref,
                 e_ref, wa_ref, wb_ref, wo_ref, g_ref, w1_ref, w2_ref, fg_ref, out_ref,
                 on2, on3, sn2, sn3, *, final):
    x1 = _merge_rows(o1_ref, o2_ref, o3_ref, s1_ref, s2_ref, s3_ref, ob_ref, ga_ref, gb_ref, x_ref,
                     e_ref, wa_ref, wb_ref, wo_ref, on2, on3, sn2, sn3)
    x2 = _mlp_rows(x1, g_ref, w1_ref, w2_ref)
    out_ref[...] = _rms(x2, fg_ref[...]) if final else x2


def _tail(oas, sts, ob, proj, x, expand, wa, wb, wo, g, w1, w2, fg, final):
    t = x.shape[0]
    per_perm = PERM_ROWS // ROW_TM

    def rows(width, cblk=0):
        return pl.BlockSpec((ROW_TM, width), lambda i: (i, cblk))

    def perm(width, d):
        return pl.BlockSpec((None, d, ROW_TM // d, width), lambda i: (i // per_perm, 0, i % per_perm, 0))

    d2, d3 = A_GROUPS[1][1], A_GROUPS[2][1]
    return pl.pallas_call(
        functools.partial(_tail_kernel, final=final),
        grid=(t // ROW_TM,),
        in_specs=[rows(A_WIDTH), perm(A_WIDTH, d2), perm(A_WIDTH, d3),
                  rows(LANES), perm(LANES, d2), perm(LANES, d3),
                  rows(B_VW), rows(D_MODEL, COL_GA // D_MODEL), rows(D_MODEL, COL_GB // D_MODEL),
                  rows(D_MODEL),
                  _resident((LANES, A_WIDTH)), _resident((A_WIDTH, D_MODEL)),
                  _resident((B_VW, D_MODEL)), _resident((D_MODEL, D_MODEL)),
                  _resident((1, D_MODEL)), _resident((D_MODEL, D_FF)), _resident((D_FF, D_MODEL)),
                  _resident((1, D_MODEL))],
        out_specs=rows(D_MODEL),
        out_shape=jax.ShapeDtypeStruct((t, D_MODEL), F32),
        scratch_shapes=[pltpu.VMEM((A_WIDTH // LANES, ROW_TM, LANES), F32),
                        pltpu.VMEM((A_WIDTH // LANES, ROW_TM, LANES), F32),
                        pltpu.VMEM((ROW_TM, LANES), F32),
                        pltpu.VMEM((ROW_TM, LANES), F32)],
        compiler_params=_cparams("parallel"),
        name="tail_final" if final else "tail",
    )(oas[0].reshape(t, A_WIDTH), oas[1], oas[2], sts[0].reshape(t, LANES), sts[1], sts[2],
      ob, proj, proj, x, expand, wa, wb, wo, g, w1, w2, fg)


def _prep_layer(l, norm_mix_g, w_in, wgf, bgf, wgb, bgb, gla_norm_g, w_branch_a, w_branch_b, w_out,
                norm_ffn_g, w_ff1, w_ff2):
    w = w_in[l]
    n_groups = len(A_GROUPS)
    a_end = 3 * n_groups * A_WIDTH
    gl1 = a_end + 2 * B_KW + 2 * B_VW + 2 * B_GATE_RANK
    col = jnp.arange(gl1)
    scale = jnp.where(col < n_groups * A_WIDTH, A_HEAD_DIM ** -0.5 * LOG2E,
                      jnp.where((col >= a_end) & (col < a_end + B_KW), B_DK ** -0.5, 1.0)).astype(w.dtype)
    w_fused = jnp.concatenate(
        [w[:, :gl1] * scale[None, :], jnp.zeros((D_MODEL, GL_PAD - 2 * B_GATE_RANK), w.dtype), w[:, gl1:]],
        axis=1)

    def gate_map(wg, row0):
        per_head = wg.reshape(B_GATE_RANK, B_HEADS, B_DK).transpose(1, 0, 2)
        return jnp.pad(per_head, ((0, 0), (row0, LANES - row0 - B_GATE_RANK), (0, 0))).astype(BF16)

    return dict(
        g_mix=norm_mix_g[l].reshape(1, D_MODEL),
        w_in=w_fused.astype(BF16),
        wgf=gate_map(wgf[l], 0),
        wgb=gate_map(wgb[l], B_GATE_RANK),
        bgf=bgf[l].reshape(B_HEADS, 1, B_DK),
        bgb=bgb[l].reshape(B_HEADS, 1, B_DK),
        gn=gla_norm_g[l].reshape(1, B_DV),
        wa=w_branch_a[l].astype(BF16),
        wb=w_branch_b[l].astype(BF16),
        wo=w_out[l].astype(BF16),
        g_ffn=norm_ffn_g[l].reshape(1, D_MODEL),
        w1=w_ff1[l].astype(BF16),
        w2=w_ff2[l].astype(BF16),
    )


def _trunk(x, layers, biases, expand, final_g):
    bsz, seq, _ = x.shape
    xf = x.reshape(bsz * seq, D_MODEL)
    for li, p in enumerate(layers):
        proj = _inproj(xf, p["g_mix"], p["w_in"])
        oas, sts = [], []
        for g, (_, dilation) in enumerate(A_GROUPS):
            o, st = _attention_group(proj, bsz, seq, g, dilation, biases[g])
            oas.append(o)
            sts.append(st)
        ob = _gla(proj, bsz, seq, p["wgf"], p["wgb"], p["bgf"], p["bgb"], p["gn"])
        xf = _tail(oas, sts, ob, proj, xf, expand, p["wa"], p["wb"], p["wo"],
                   p["g_ffn"], p["w1"], p["w2"], final_g, final=(li == len(layers) - 1))
    return xf.reshape(bsz, seq, D_MODEL)


def kernel(x_prompt, x_sample, norm_mix_g, w_in, gla_w_gate_fwd, gla_b_gate_fwd, gla_w_gate_bwd,
           gla_b_gate_bwd, gla_norm_g, w_branch_a, w_branch_b, w_out, norm_ffn_g, w_ff1, w_ff2,
           final_norm_g):
    layers = [_prep_layer(l, norm_mix_g, w_in, gla_w_gate_fwd, gla_b_gate_fwd, gla_w_gate_bwd,
                          gla_b_gate_bwd, gla_norm_g, w_branch_a, w_branch_b, w_out, norm_ffn_g,
                          w_ff1, w_ff2) for l in range(DEPTH)]
    n_heads = len(A_GROUPS) * A_HEADS
    slopes = jnp.exp2(-8.0 * jnp.arange(1, n_heads + 1, dtype=F32) / n_heads)
    biases = [_attn_bias(slopes[g * A_HEADS:(g + 1) * A_HEADS], dilation)
              for g, (_, dilation) in enumerate(A_GROUPS)]
    expand = (jnp.arange(LANES)[:, None] == jnp.arange(A_WIDTH)[None, :] // A_HEAD_DIM).astype(BF16)
    final_g = final_norm_g.reshape(1, D_MODEL)
    return (_trunk(x_prompt, layers, biases, expand, final_g),
            _trunk(x_sample, layers, biases, expand, final_g))
```

```python
import functools

import jax
import jax.numpy as jnp
from jax import lax
from jax.experimental import pallas as pl
from jax.experimental.pallas import tpu as pltpu

F32 = jnp.float32
BF16 = jnp.bfloat16

D_MODEL = 1024
DEPTH = 2
A_GROUPS = ((128, 1), (512, 4), (2048, 16))
A_HEADS = 8
A_HEAD_DIM = 64
A_WIDTH = A_HEADS * A_HEAD_DIM
A_QKV_W = 3 * A_WIDTH
A_HALF = 64
B_HEADS = 4
B_DK = 128
B_DV = 256
B_KW = B_HEADS * B_DK
B_VW = B_HEADS * B_DV
B_GATE_RANK = 16
B_GATE_TAU = 16.0
B_CHUNK = 64
D_FF = 4 * D_MODEL
EPS = 1e-6
NEG = -1e30
LOG2E = 1.4426950408889634
LANES = 128

COL_A = 0
COL_QB = COL_A + len(A_GROUPS) * A_QKV_W
COL_KB = COL_QB + B_KW
COL_VB = COL_KB + B_KW
COL_RB = COL_VB + B_VW
COL_GL = COL_RB + B_VW
GL_PAD = 512
COL_GA = COL_GL + GL_PAD
COL_GB = COL_GA + D_MODEL
PROJ_W = COL_GB + D_MODEL

VMEM_LIMIT_BYTES = 56 * 1024 * 1024

PERM_ROWS = 1024
IN_TM = PERM_ROWS
IN_TN = 2560
IN_SUB = 512
ROW_TM = 512
ATT_ROWS = 2048
ATT_SB = 128
ATT_KW = ATT_SB + 2 * A_HALF
GLA_ROWS = 2 * B_CHUNK
GLA_UNROLL = 16


def _cparams(*sem):
    return pltpu.CompilerParams(dimension_semantics=sem, vmem_limit_bytes=VMEM_LIMIT_BYTES)


def _resident(shape):
    nd = len(shape)
    return pl.BlockSpec(shape, lambda *_: (0,) * nd, pipeline_mode=pl.Buffered(1))


def _rms(x, g):
    return x * lax.rsqrt(jnp.mean(x * x, axis=-1, keepdims=True) + EPS) * g


def _sigmoid(t):
    return 1.0 / (1.0 + jnp.exp2(t * (-LOG2E)))


def _inproj_kernel(x_ref, g_ref, w_ref, o_ref, hf_ref, hs_ref):
    j = pl.program_id(1)
    n_groups = len(A_GROUPS)
    n_chunks = IN_TN // IN_SUB

    def project(c, k):
        sl = slice(c * IN_SUB, (c + 1) * IN_SUB)
        o_ref[:, sl] = jnp.dot(hs_ref[k], w_ref[:, sl], preferred_element_type=F32).astype(BF16)

    @pl.when(j == 0)
    def _():
        h = _rms(x_ref[...], g_ref[...])
        hs_ref[0] = h.astype(BF16)
        groups = [c % n_groups for c in range(n_chunks)]
        for c in range(n_chunks):
            if groups[c] == 0:
                project(c, 0)
        d1, d2 = A_GROUPS[1][1], A_GROUPS[2][1]
        n1, n2, ratio = IN_TM // d1, IN_TM // d2, d2 // d1
        for s in range(D_MODEL // LANES):
            sl = slice(s * LANES, (s + 1) * LANES)
            hf_ref[0, s] = h[:, sl]
            for r in range(d1):
                part = hf_ref[0, s, pl.ds(r, n1, stride=d1), :]
                hf_ref[1, s, r * n1:(r + 1) * n1, :] = part
                hs_ref[1, r * n1:(r + 1) * n1, sl] = part.astype(BF16)
        for c in range(n_chunks):
            if groups[c] == 1:
                project(c, 1)
        for s in range(D_MODEL // LANES):
            sl = slice(s * LANES, (s + 1) * LANES)
            for r in range(d2):
                r1, q = r % d1, r // d1
                hs_ref[2, r * n2:(r + 1) * n2, sl] = (
                    hf_ref[1, s, pl.ds(r1 * n1 + q, n2, stride=ratio), :].astype(BF16))
        for c in range(n_chunks):
            if groups[c] == 2:
                project(c, 2)

    @pl.when(j > 0)
    def _():
        for c in range(n_chunks):
            cc = j * n_chunks + c
            project(c, jnp.where(cc < 3 * n_groups, cc % n_groups, 0))


def _inproj(x, g, w):
    assert IN_SUB == A_WIDTH and IN_TN <= COL_QB - COL_A
    t = x.shape[0]
    return pl.pallas_call(
        _inproj_kernel,
        grid=(t // IN_TM, PROJ_W // IN_TN),
        in_specs=[
            pl.BlockSpec((IN_TM, D_MODEL), lambda i, j: (i, 0)),
            pl.BlockSpec((1, D_MODEL), lambda i, j: (0, 0)),
            pl.BlockSpec((D_MODEL, IN_TN), lambda i, j: (0, j)),
        ],
        out_specs=pl.BlockSpec((IN_TM, IN_TN), lambda i, j: (i, j)),
        out_shape=jax.ShapeDtypeStruct((t, PROJ_W), BF16),
        scratch_shapes=[pltpu.VMEM((2, D_MODEL // LANES, IN_TM, LANES), F32),
                        pltpu.VMEM((len(A_GROUPS), IN_TM, D_MODEL), BF16)],
        compiler_params=_cparams("parallel", "arbitrary"),
        name="inproj",
    )(x, g, w)


def _attn_kernel(q_ref, kp_ref, k_ref, kn_ref, vp_ref, v_ref, vn_ref, bm_ref, o_ref, st_ref,
                 *, tq, sub, run, n_res):
    i = pl.program_id(2)
    lane = lax.broadcasted_iota(jnp.int32, (ATT_SB, LANES), 1)
    low = lane < A_HEAD_DIM
    nt = (((1,), (1,)), ((), ()))

    def pieces(a, b):
        out, r = [], a
        while r < b:
            c = r // run
            e = min(b, (c + 1) * run)
            out.append((c, r - c * run, e - c * run))
            r = e
        return out

    def read(ref, rr, a, b, ls):
        parts = [ref[c, rr, s:e, ls] for c, s, e in pieces(a, b)]
        return parts[0] if len(parts) == 1 else jnp.concatenate(parts, axis=0)

    def window(prev_ref, ref, next_ref, rr, a, ls):
        parts = []
        if a < 0:
            parts.append(prev_ref[rr, :, ls])
        parts.append(read(ref, rr, max(a, 0), min(a + ATT_KW, tq), ls))
        if a + ATT_KW > tq:
            parts.append(next_ref[rr, :, ls])
        return jnp.concatenate(parts, axis=0)

    def write(ref, rr, a, ls, val):
        off = 0
        for c, s, e in pieces(a, a + ATT_SB):
            ref[c, rr, s:e, ls] = val[off:off + e - s]
            off += e - s

    for rr, jq in [(rr, jq) for rr in range(n_res) for jq in range(tq // ATT_SB)]:
        r0 = jq * ATT_SB
        kbase = i * tq + r0 - A_HALF
        variant = (kbase < 0).astype(jnp.int32) + 2 * (kbase + ATT_KW > sub).astype(jnp.int32)
        vws, scores = [], []
        for p in range(A_HEADS // 2):
            ls = slice(p * LANES, (p + 1) * LANES)
            qp = read(q_ref, rr, r0, r0 + ATT_SB, ls)
            kw = window(kp_ref, k_ref, kn_ref, rr, r0 - A_HALF, ls)
            vws.append(window(vp_ref, v_ref, vn_ref, rr, r0 - A_HALF, ls))
            zero = jnp.zeros_like(qp)
            scores.append(lax.dot_general(jnp.where(low, qp, zero), kw, nt, preferred_element_type=F32))
            scores.append(lax.dot_general(jnp.where(low, zero, qp), kw, nt, preferred_element_type=F32))
        probs = []
        st = jnp.ones((ATT_SB, LANES), F32)
        for h, s in enumerate(scores):
            t = s + bm_ref[variant * A_HEADS + h]
            m = jnp.max(t, axis=-1, keepdims=True)
            pe = jnp.exp2(t - m)
            probs.append(pe.astype(BF16))
            st = jnp.where(lane == h, m, st)
            st = jnp.where(lane == A_HEADS + h, jnp.sum(pe, axis=-1, keepdims=True), st)
        pvs = [jnp.dot(pe, vws[h // 2], preferred_element_type=F32) for h, pe in enumerate(probs)]
        for p in range(A_HEADS // 2):
            write(o_ref, rr, r0, slice(p * LANES, (p + 1) * LANES),
                  jnp.where(low, pvs[2 * p], pvs[2 * p + 1]).astype(BF16))
        write(st_ref, rr, r0, slice(0, LANES), st)


def _attn_bias(slopes, dilation):
    qi = jnp.arange(ATT_SB)[:, None]
    ci = jnp.arange(ATT_KW)[None, :]
    delta = ci - A_HALF - qi
    dist = (jnp.abs(delta) * dilation).astype(F32)
    bias = -slopes[:, None, None] * dist[None] * LOG2E
    band = jnp.abs(delta) <= A_HALF
    tables = []
    for v in range(4):
        ok = band
        if v & 1:
            ok = ok & (ci >= A_HALF)
        if v & 2:
            ok = ok & (ci < ATT_KW - A_HALF)
        tables.append(jnp.where(ok[None], bias, NEG))
    return jnp.concatenate(tables, axis=0)


def _attention_group(proj, bsz, seq, g, dilation, bm):
    sub = seq // dilation
    run = PERM_ROWS // dilation
    tq = min(ATT_ROWS, sub)
    assert tq % run == 0 and sub % tq == 0
    n_runs = tq // run
    n_res = min(dilation, ATT_ROWS // tq)
    n_tiles = seq // PERM_ROWS
    view = proj.reshape(bsz, n_tiles, dilation, run, PROJ_W)
    n_groups = len(A_GROUPS)
    cq = COL_A // A_WIDTH + g
    ck = cq + n_groups
    cv = ck + n_groups
    halo_per_run = run // A_HALF
    n_halo = sub // A_HALF
    halo_per_tq = tq // A_HALF

    def main(width, c):
        return pl.BlockSpec((None, n_runs, n_res, run, width), lambda b, r, i: (b, i, r, 0, c))

    def halo(c, which):
        def index(b, r, i):
            if which == "prev":
                hb = jnp.maximum(i * halo_per_tq - 1, 0)
            else:
                hb = jnp.minimum((i + 1) * halo_per_tq, n_halo - 1)
            return (b, hb // halo_per_run, r, hb % halo_per_run, c)
        return pl.BlockSpec((None, None, n_res, A_HALF, A_WIDTH), index)

    o, st = pl.pallas_call(
        functools.partial(_attn_kernel, tq=tq, sub=sub, run=run, n_res=n_res),
        grid=(bsz, dilation // n_res, sub // tq),
        in_specs=[main(A_WIDTH, cq), halo(ck, "prev"), main(A_WIDTH, ck), halo(ck, "next"),
                  halo(cv, "prev"), main(A_WIDTH, cv), halo(cv, "next"),
                  _resident((4 * A_HEADS, ATT_SB, ATT_KW))],
        out_specs=[main(A_WIDTH, 0), main(LANES, 0)],
        out_shape=[jax.ShapeDtypeStruct((bsz, n_tiles, dilation, run, A_WIDTH), BF16),
                   jax.ShapeDtypeStruct((bsz, n_tiles, dilation, run, LANES), F32)],
        compiler_params=_cparams("parallel", "parallel", "arbitrary"),
        name=f"attn_d{dilation}",
    )(view, view, view, view, view, view, view, bm)
    n_perm = bsz * n_tiles
    return (o.reshape(n_perm, dilation, run, A_WIDTH), st.reshape(n_perm, dilation, run, LANES))


def _gla_kernel(q_ref, k_ref, v_ref, r_ref, gl_ref, wgf_ref, wgb_ref, bf_ref, bb_ref, gn_ref,
                o_ref, oacc, sf, sb, *, seq):
    c = GLA_ROWS
    half = c // 2
    n_blocks = seq // c
    u = min(GLA_UNROLL, n_blocks // 2)
    sf[...] = jnp.zeros_like(sf)
    sb[...] = jnp.zeros_like(sb)
    ri = lax.broadcasted_iota(jnp.int32, (c, c), 0)
    ci = lax.broadcasted_iota(jnp.int32, (c, c), 1)
    tril = (ci <= ri).astype(BF16)
    tril2 = jnp.concatenate([tril, tril], axis=1)
    nt = (((1,), (1,)), ((), ()))

    def log2_decay(z):
        e = jnp.exp2(jnp.abs(z) * (-LOG2E))
        return (jnp.minimum(z, 0.0) - jnp.log(1.0 + e)) * (LOG2E / B_GATE_TAU)

    def prefix_sum(la):
        hi = la.astype(BF16)
        lo = (la - hi.astype(F32)).astype(BF16)
        return jnp.dot(tril2, jnp.concatenate([hi, lo], axis=0), preferred_element_type=F32)

    def exponents(la, cs, forward):
        tot = cs[c - 1:c, :]
        if forward:
            ref = cs[half - 1:half, :]
            return cs - ref, ref, tot
        suf = tot - cs + la
        ref = suf[half:half + 1, :]
        return suf - ref, ref, tot

    def body(nb, finish):
        rows_f = [pl.multiple_of((nb * u + j) * c, c) for j in range(u)]
        rows_b = [pl.multiple_of((n_blocks - 1 - nb * u - j) * c, c) for j in range(u)]
        chains = [(r, True) for r in rows_f] + [(r, False) for r in rows_b]
        zs = [jnp.dot(gl_ref[0, pl.ds(r, c), :], (wgf_ref if fw else wgb_ref)[0],
                      preferred_element_type=F32) + (bf_ref if fw else bb_ref)[0]
              for r, fw in chains]
        las = [log2_decay(z) for z in zs]
        css = [prefix_sum(la) for la in las]
        qes, kes, qis, kds, decs = [], [], [], [], []
        for (r, fw), la, cs in zip(chains, las, css):
            rel, ref, tot = exponents(la, cs, fw)
            qp = q_ref[0, pl.ds(r, c), :].astype(F32) * jnp.exp2(rel)
            kp = k_ref[0, pl.ds(r, c), :].astype(F32) * jnp.exp2(-rel)
            qes.append(qp.astype(BF16))
            kes.append(kp.astype(BF16))
            qis.append((qp * jnp.exp2(ref)).astype(BF16))
            kds.append(kp * jnp.exp2(tot - ref))
            decs.append(jnp.exp2(tot))
        atts = [lax.dot_general(qe, ke, nt, preferred_element_type=F32) for qe, ke in zip(qes, kes)]
        atts = [jnp.where((ci <= ri) if fw else (ci > ri), a, 0.0).astype(BF16)
                for (_, fw), a in zip(chains, atts)]
        vs = [v_ref[0, pl.ds(r, c), :] for r, _ in chains]
        ups = [jnp.dot(kd.T.astype(BF16), v, preferred_element_type=F32) for kd, v in zip(kds, vs)]
        states = []
        for st_ref, first in ((sf, 0), (sb, u)):
            st = st_ref[...]
            for idx in range(first, first + u):
                states.append(st.astype(BF16))
                dcol = jnp.broadcast_to(decs[idx], (B_DK, B_DK)).T
                st = st * jnp.concatenate([dcol, dcol], axis=1) + ups[idx]
            st_ref[...] = st
        outs = [jnp.dot(jnp.concatenate([qi, a], axis=1), jnp.concatenate([st, v], axis=0),
                        preferred_element_type=F32)
                for a, v, qi, st in zip(atts, vs, qis, states)]
        for (r, _), o in zip(chains, outs):
            if finish:
                y = _rms(o + oacc[pl.ds(r, c), :], gn_ref[...])
                rr = r_ref[0, pl.ds(r, c), :].astype(F32)
                o_ref[0, pl.ds(r, c), :] = (y * (rr * _sigmoid(rr))).astype(BF16)
            else:
                oacc[pl.ds(r, c), :] = o

    def first_half(nb, carry):
        body(nb, False)
        return carry

    def second_half(nb, carry):
        body(nb, True)
        return carry

    n_bodies = n_blocks // u
    lax.fori_loop(0, n_bodies // 2, first_half, 0)
    lax.fori_loop(n_bodies // 2, n_bodies, second_half, 0)


def _gla(proj, bsz, seq, wgf, wgb, bgf, bgb, gn):
    view = proj.reshape(bsz, seq, PROJ_W)

    def cols(width, col0):
        return pl.BlockSpec((1, seq, width), lambda b, h: (b, 0, col0 // width + h))

    def per_head(shape):
        return pl.BlockSpec((1,) + shape, lambda b, h: (h, 0, 0))

    out = pl.pallas_call(
        functools.partial(_gla_kernel, seq=seq),
        grid=(bsz, B_HEADS),
        in_specs=[cols(B_DK, COL_QB), cols(B_DK, COL_KB), cols(B_DV, COL_VB), cols(B_DV, COL_RB),
                  pl.BlockSpec((1, seq, LANES), lambda b, h: (b, 0, COL_GL // LANES)),
                  per_head((LANES, B_DK)), per_head((LANES, B_DK)),
                  per_head((1, B_DK)), per_head((1, B_DK)),
                  pl.BlockSpec((1, B_DV), lambda b, h: (0, 0))],
        out_specs=pl.BlockSpec((1, seq, B_DV), lambda b, h: (b, 0, h)),
        out_shape=jax.ShapeDtypeStruct((bsz, seq, B_VW), BF16),
        scratch_shapes=[pltpu.VMEM((seq, B_DV), F32),
                        pltpu.VMEM((B_DK, B_DV), F32),
                        pltpu.VMEM((B_DK, B_DV), F32)],
        compiler_params=_cparams("parallel", "arbitrary"),
        name="gla",
    )(view, view, view, view, view, wgf, wgb, bgf, bgb, gn)
    return out.reshape(bsz * seq, B_VW)


def _merge_rows(o1_ref, o2_ref, o3_ref, s1_ref, s2_ref, s3_ref, ob_ref, ga_ref, gb_ref, x_ref,
                e_ref, wa_ref, wb_ref, wo_ref, on2, on3, sn2, sn3):
    n_slabs = A_WIDTH // LANES
    for (_, d), o_ref, s_ref, on, sn in ((A_GROUPS[1], o2_ref, s2_ref, on2, sn2),
                                        (A_GROUPS[2], o3_ref, s3_ref, on3, sn3)):
        n = ROW_TM // d
        for r in range(d):
            sn[pl.ds(r, n, stride=d), :] = s_ref[r]
            for s in range(n_slabs):
                on[s, pl.ds(r, n, stride=d), :] = o_ref[r, :, s * LANES:(s + 1) * LANES].astype(F32)

    ms = (s1_ref[...], sn2[...], sn3[...])
    ls = [pltpu.roll(m, LANES - A_HEADS, axis=1) for m in ms]
    top = jnp.maximum(jnp.maximum(ms[0], ms[1]), ms[2])
    es = [jnp.exp2(m - top) for m in ms]
    inv = 1.0 / (es[0] * ls[0] + es[1] * ls[1] + es[2] * ls[2])
    ws = [jnp.dot((e * inv).astype(BF16), e_ref[...], preferred_element_type=F32) for e in es]
    slabs = []
    for s in range(n_slabs):
        sl = slice(s * LANES, (s + 1) * LANES)
        slabs.append(o1_ref[:, sl].astype(F32) * ws[0][:, sl]
                     + on2[s] * ws[1][:, sl] + on3[s] * ws[2][:, sl])
    oa = jnp.concatenate(slabs, axis=1)
    ya = jnp.dot(oa.astype(BF16), wa_ref[...], preferred_element_type=F32)
    yb = jnp.dot(ob_ref[...], wb_ref[...], preferred_element_type=F32)
    merged = _sigmoid(ga_ref[...].astype(F32)) * ya + _sigmoid(gb_ref[...].astype(F32)) * yb
    return x_ref[...] + jnp.dot(merged.astype(BF16), wo_ref[...], preferred_element_type=F32)


def _mlp_rows(x, g_ref, w1_ref, w2_ref):
    h = (x * g_ref[...]).astype(BF16)
    acc = jnp.zeros_like(x)
    for c in range(D_FF // D_MODEL):
        sl = slice(c * D_MODEL, (c + 1) * D_MODEL)
        u = jnp.maximum(jnp.dot(h, w1_ref[:, sl], preferred_element_type=F32), 0.0)
        acc = acc + jnp.dot((u * u).astype(BF16), w2_ref[sl, :], preferred_element_type=F32)
    return x + acc * (1.0 / (jnp.mean(x * x, axis=-1, keepdims=True) + EPS))


def _tail_kernel(o1_ref, o2_ref, o3_ref, s1_ref, s2_ref, s3_ref, ob_ref, ga_ref, gb_ref, x_ref,
                 e_ref, wa_ref, wb_ref, wo_ref, g_ref, w1_ref, w2_ref, fg_ref, out_ref,
                 on2, on3, sn2, sn3, *, final):
    x1 = _merge_rows(o1_ref, o2_ref, o3_ref, s1_ref, s2_ref, s3_ref, ob_ref, ga_ref, gb_ref, x_ref,
                     e_ref, wa_ref, wb_ref, wo_ref, on2, on3, sn2, sn3)
    x2 = _mlp_rows(x1, g_ref, w1_ref, w2_ref)
    out_ref[...] = _rms(x2, fg_ref[...]) if final else x2


def _tail(oas, sts, ob, proj, x, expand, wa, wb, wo, g, w1, w2, fg, final):
    t = x.shape[0]
    per_perm = PERM_ROWS // ROW_TM

    def rows(width, cblk=0):
        return pl.BlockSpec((ROW_TM, width), lambda i: (i, cblk))

    def perm(width, d):
        return pl.BlockSpec((None, d, ROW_TM // d, width), lambda i: (i // per_perm, 0, i % per_perm, 0))

    d2, d3 = A_GROUPS[1][1], A_GROUPS[2][1]
    return pl.pallas_call(
        functools.partial(_tail_kernel, final=final),
        grid=(t // ROW_TM,),
        in_specs=[rows(A_WIDTH), perm(A_WIDTH, d2), perm(A_WIDTH, d3),
                  rows(LANES), perm(LANES, d2), perm(LANES, d3),
                  rows(B_VW), rows(D_MODEL, COL_GA // D_MODEL), rows(D_MODEL, COL_GB // D_MODEL),
                  rows(D_MODEL),
                  _resident((LANES, A_WIDTH)), _resident((A_WIDTH, D_MODEL)),
                  _resident((B_VW, D_MODEL)), _resident((D_MODEL, D_MODEL)),
                  _resident((1, D_MODEL)), _resident((D_MODEL, D_FF)), _resident((D_FF, D_MODEL)),
                  _resident((1, D_MODEL))],
        out_specs=rows(D_MODEL),
        out_shape=jax.ShapeDtypeStruct((t, D_MODEL), F32),
        scratch_shapes=[pltpu.VMEM((A_WIDTH // LANES, ROW_TM, LANES), F32),
                        pltpu.VMEM((A_WIDTH // LANES, ROW_TM, LANES), F32),
                        pltpu.VMEM((ROW_TM, LANES), F32),
                        pltpu.VMEM((ROW_TM, LANES), F32)],
        compiler_params=_cparams("parallel"),
        name="tail_final" if final else "tail",
    )(oas[0].reshape(t, A_WIDTH), oas[1], oas[2], sts[0].reshape(t, LANES), sts[1], sts[2],
      ob, proj, proj, x, expand, wa, wb, wo, g, w1, w2, fg)


def _prep_layer(l, norm_mix_g, w_in, wgf, bgf, wgb, bgb, gla_norm_g, w_branch_a, w_branch_b, w_out,
                norm_ffn_g, w_ff1, w_ff2):
    w = w_in[l]
    n_groups = len(A_GROUPS)
    a_end = 3 * n_groups * A_WIDTH
    gl1 = a_end + 2 * B_KW + 2 * B_VW + 2 * B_GATE_RANK
    col = jnp.arange(gl1)
    scale = jnp.where(col < n_groups * A_WIDTH, A_HEAD_DIM ** -0.5 * LOG2E,
                      jnp.where((col >= a_end) & (col < a_end + B_KW), B_DK ** -0.5, 1.0)).astype(w.dtype)
    w_fused = jnp.concatenate(
        [w[:, :gl1] * scale[None, :], jnp.zeros((D_MODEL, GL_PAD - 2 * B_GATE_RANK), w.dtype), w[:, gl1:]],
        axis=1)

    def gate_map(wg, row0):
        per_head = wg.reshape(B_GATE_RANK, B_HEADS, B_DK).transpose(1, 0, 2)
        return jnp.pad(per_head, ((0, 0), (row0, LANES - row0 - B_GATE_RANK), (0, 0))).astype(BF16)

    return dict(
        g_mix=norm_mix_g[l].reshape(1, D_MODEL),
        w_in=w_fused.astype(BF16),
        wgf=gate_map(wgf[l], 0),
        wgb=gate_map(wgb[l], B_GATE_RANK),
        bgf=bgf[l].reshape(B_HEADS, 1, B_DK),
        bgb=bgb[l].reshape(B_HEADS, 1, B_DK),
        gn=gla_norm_g[l].reshape(1, B_DV),
        wa=w_branch_a[l].astype(BF16),
        wb=w_branch_b[l].astype(BF16),
        wo=w_out[l].astype(BF16),
        g_ffn=norm_ffn_g[l].reshape(1, D_MODEL),
        w1=w_ff1[l].astype(BF16),
        w2=w_ff2[l].astype(BF16),
    )


def _trunk(x, layers, biases, expand, final_g):
    bsz, seq, _ = x.shape
    xf = x.reshape(bsz * seq, D_MODEL)
    for li, p in enumerate(layers):
        proj = _inproj(xf, p["g_mix"], p["w_in"])
        oas, sts = [], []
        for g, (_, dilation) in enumerate(A_GROUPS):
            o, st = _attention_group(proj, bsz, seq, g, dilation, biases[g])
            oas.append(o)
            sts.append(st)
        ob = _gla(proj, bsz, seq, p["wgf"], p["wgb"], p["bgf"], p["bgb"], p["gn"])
        xf = _tail(oas, sts, ob, proj, xf, expand, p["wa"], p["wb"], p["wo"],
                   p["g_ffn"], p["w1"], p["w2"], final_g, final=(li == len(layers) - 1))
    return xf.reshape(bsz, seq, D_MODEL)


def kernel(x_prompt, x_sample, norm_mix_g, w_in, gla_w_gate_fwd, gla_b_gate_fwd, gla_w_gate_bwd,
           gla_b_gate_bwd, gla_norm_g, w_branch_a, w_branch_b, w_out, norm_ffn_g, w_ff1, w_ff2,
           final_norm_g):
    layers = [_prep_layer(l, norm_mix_g, w_in, gla_w_gate_fwd, gla_b_gate_fwd, gla_w_gate_bwd,
                          gla_b_gate_bwd, gla_norm_g, w_branch_a, w_branch_b, w_out, norm_ffn_g,
                          w_ff1, w_ff2) for l in range(DEPTH)]
    n_heads = len(A_GROUPS) * A_HEADS
    slopes = jnp.exp2(-8.0 * jnp.arange(1, n_heads + 1, dtype=F32) / n_heads)
    biases = [_attn_bias(slopes[g * A_HEADS:(g + 1) * A_HEADS], dilation)
              for g, (_, dilation) in enumerate(A_GROUPS)]
    expand = (jnp.arange(LANES)[:, None] == jnp.arange(A_WIDTH)[None, :] // A_HEAD_DIM).astype(BF16)
    final_g = final_norm_g.reshape(1, D_MODEL)
    return (_trunk(x_prompt, layers, biases, expand, final_g),
            _trunk(x_sample, layers, biases, expand, final_g))
```

```python
import functools

import jax
import jax.numpy as jnp
from jax import lax
from jax.experimental import pallas as pl
from jax.experimental.pallas import tpu as pltpu

F32 = jnp.float32
BF16 = jnp.bfloat16

D_MODEL = 1024
DEPTH = 2
A_GROUPS = ((128, 1), (512, 4), (2048, 16))
A_HEADS = 8
A_HEAD_DIM = 64
A_WIDTH = A_HEADS * A_HEAD_DIM
A_QKV_W = 3 * A_WIDTH
A_HALF = 64
B_HEADS = 4
B_DK = 128
B_DV = 256
B_KW = B_HEADS * B_DK
B_VW = B_HEADS * B_DV
B_GATE_RANK = 16
B_GATE_TAU = 16.0
B_CHUNK = 64
D_FF = 4 * D_MODEL
EPS = 1e-6
NEG = -1e30
LOG2E = 1.4426950408889634
LANES = 128

COL_A = 0
COL_QB = COL_A + len(A_GROUPS) * A_QKV_W
COL_KB = COL_QB + B_KW
COL_VB = COL_KB + B_KW
COL_RB = COL_VB + B_VW
COL_GL = COL_RB + B_VW
GL_PAD = 512
COL_GA = COL_GL + GL_PAD
COL_GB = COL_GA + D_MODEL
PROJ_W = COL_GB + D_MODEL

VMEM_LIMIT_BYTES = 56 * 1024 * 1024

PERM_ROWS = 1024
IN_TM = PERM_ROWS
IN_TN = 2560
IN_SUB = 512
ROW_TM = 512
ATT_ROWS = 2048
ATT_SB = 128
ATT_KW = ATT_SB + 2 * A_HALF
GLA_ROWS = 2 * B_CHUNK
GLA_UNROLL = 16


def _cparams(*sem):
    return pltpu.CompilerParams(dimension_semantics=sem, vmem_limit_bytes=VMEM_LIMIT_BYTES)


def _resident(shape):
    nd = len(shape)
    return pl.BlockSpec(shape, lambda *_: (0,) * nd, pipeline_mode=pl.Buffered(1))


def _rms(x, g):
    return x * lax.rsqrt(jnp.mean(x * x, axis=-1, keepdims=True) + EPS) * g


def _sigmoid(t):
    return 1.0 / (1.0 + jnp.exp2(t * (-LOG2E)))


def _inproj_kernel(x_ref, g_ref, w_ref, o_ref, hf_ref, hs_ref):
    j = pl.program_id(1)
    n_groups = len(A_GROUPS)
    n_chunks = IN_TN // IN_SUB

    def project(c, k):
        sl = slice(c * IN_SUB, (c + 1) * IN_SUB)
        o_ref[:, sl] = jnp.dot(hs_ref[k], w_ref[:, sl], preferred_element_type=F32).astype(BF16)

    @pl.when(j == 0)
    def _():
        h = _rms(x_ref[...], g_ref[...])
        hs_ref[0] = h.astype(BF16)
        groups = [c % n_groups for c in range(n_chunks)]
        for c in range(n_chunks):
            if groups[c] == 0:
                project(c, 0)
        d1, d2 = A_GROUPS[1][1], A_GROUPS[2][1]
        n1, n2, ratio = IN_TM // d1, IN_TM // d2, d2 // d1
        for s in range(D_MODEL // LANES):
            sl = slice(s * LANES, (s + 1) * LANES)
            hf_ref[0, s] = h[:, sl]
            for r in range(d1):
                part = hf_ref[0, s, pl.ds(r, n1, stride=d1), :]
                hf_ref[1, s, r * n1:(r + 1) * n1, :] = part
                hs_ref[1, r * n1:(r + 1) * n1, sl] = part.astype(BF16)
        for c in range(n_chunks):
            if groups[c] == 1:
                project(c, 1)
        for s in range(D_MODEL // LANES):
            sl = slice(s * LANES, (s + 1) * LANES)
            for r in range(d2):
                r1, q = r % d1, r // d1
                hs_ref[2, r * n2:(r + 1) * n2, sl] = (
                    hf_ref[1, s, pl.ds(r1 * n1 + q, n2, stride=ratio), :].astype(BF16))
        for c in range(n_chunks):
            if groups[c] == 2:
                project(c, 2)

    @pl.when(j > 0)
    def _():
        for c in range(n_chunks):
            cc = j * n_chunks + c
            project(c, jnp.where(cc < 3 * n_groups, cc % n_groups, 0))


def _inproj(x, g, w):
    assert IN_SUB == A_WIDTH and IN_TN <= COL_QB - COL_A
    t = x.shape[0]
    return pl.pallas_call(
        _inproj_kernel,
        grid=(t // IN_TM, PROJ_W // IN_TN),
        in_specs=[
            pl.BlockSpec((IN_TM, D_MODEL), lambda i, j: (i, 0)),
            pl.BlockSpec((1, D_MODEL), lambda i, j: (0, 0)),
            pl.BlockSpec((D_MODEL, IN_TN), lambda i, j: (0, j)),
        ],
        out_specs=pl.BlockSpec((IN_TM, IN_TN), lambda i, j: (i, j)),
        out_shape=jax.ShapeDtypeStruct((t, PROJ_W), BF16),
        scratch_shapes=[pltpu.VMEM((2, D_MODEL // LANES, IN_TM, LANES), F32),
                        pltpu.VMEM((len(A_GROUPS), IN_TM, D_MODEL), BF16)],
        compiler_params=_cparams("parallel", "arbitrary"),
        name="inproj",
    )(x, g, w)


def _attn_kernel(q_ref, kp_ref, k_ref, kn_ref, vp_ref, v_ref, vn_ref, bm_ref, o_ref, st_ref,
                 *, tq, sub, run, n_res):
    i = pl.program_id(2)
    lane = lax.broadcasted_iota(jnp.int32, (ATT_SB, LANES), 1)
    low = lane < A_HEAD_DIM
    nt = (((1,), (1,)), ((), ()))

    def pieces(a, b):
        out, r = [], a
        while r < b:
            c = r // run
            e = min(b, (c + 1) * run)
            out.append((c, r - c * run, e - c * run))
            r = e
        return out

    def read(ref, rr, a, b, ls):
        parts = [ref[c, rr, s:e, ls] for c, s, e in pieces(a, b)]
        return parts[0] if len(parts) == 1 else jnp.concatenate(parts, axis=0)

    def window(prev_ref, ref, next_ref, rr, a, ls):
        parts = []
        if a < 0:
            parts.append(prev_ref[rr, :, ls])
        parts.append(read(ref, rr, max(a, 0), min(a + ATT_KW, tq), ls))
        if a + ATT_KW > tq:
            parts.append(next_ref[rr, :, ls])
        return jnp.concatenate(parts, axis=0)

    def write(ref, rr, a, ls, val):
        off = 0
        for c, s, e in pieces(a, a + ATT_SB):
            ref[c, rr, s:e, ls] = val[off:off + e - s]
            off += e - s

    for rr, jq in [(rr, jq) for rr in range(n_res) for jq in range(tq // ATT_SB)]:
        r0 = jq * ATT_SB
        kbase = i * tq + r0 - A_HALF
        variant = (kbase < 0).astype(jnp.int32) + 2 * (kbase + ATT_KW > sub).astype(jnp.int32)
        vws, scores = [], []
        for p in range(A_HEADS // 2):
            ls = slice(p * LANES, (p + 1) * LANES)
            qp = read(q_ref, rr, r0, r0 + ATT_SB, ls)
            kw = window(kp_ref, k_ref, kn_ref, rr, r0 - A_HALF, ls)
            vws.append(window(vp_ref, v_ref, vn_ref, rr, r0 - A_HALF, ls))
            zero = jnp.zeros_like(qp)
            scores.append(lax.dot_general(jnp.where(low, qp, zero), kw, nt, preferred_element_type=F32))
            scores.append(lax.dot_general(jnp.where(low, zero, qp), kw, nt, preferred_element_type=F32))
        probs = []
        st = jnp.ones((ATT_SB, LANES), F32)
        for h, s in enumerate(scores):
            t = s + bm_ref[variant * A_HEADS + h]
            m = jnp.max(t, axis=-1, keepdims=True)
            pe = jnp.exp2(t - m)
            probs.append(pe.astype(BF16))
            st = jnp.where(lane == h, m, st)
            st = jnp.where(lane == A_HEADS + h, jnp.sum(pe, axis=-1, keepdims=True), st)
        pvs = [jnp.dot(pe, vws[h // 2], preferred_element_type=F32) for h, pe in enumerate(probs)]
        for p in range(A_HEADS // 2):
            write(o_ref, rr, r0, slice(p * LANES, (p + 1) * LANES),
                  jnp.where(low, pvs[2 * p], pvs[2 * p + 1]).astype(BF16))
        write(st_ref, rr, r0, slice(0, LANES), st)


def _attn_bias(slopes, dilation):
    qi = jnp.arange(ATT_SB)[:, None]
    ci = jnp.arange(ATT_KW)[None, :]
    delta = ci - A_HALF - qi
    dist = (jnp.abs(delta) * dilation).astype(F32)
    bias = -slopes[:, None, None] * dist[None] * LOG2E
    band = jnp.abs(delta) <= A_HALF
    tables = []
    for v in range(4):
        ok = band
        if v & 1:
            ok = ok & (ci >= A_HALF)
        if v & 2:
            ok = ok & (ci < ATT_KW - A_HALF)
        tables.append(jnp.where(ok[None], bias, NEG))
    return jnp.concatenate(tables, axis=0)


def _attention_group(proj, bsz, seq, g, dilation, bm):
    sub = seq // dilation
    run = PERM_ROWS // dilation
    tq = min(ATT_ROWS, sub)
    assert tq % run == 0 and sub % tq == 0
    n_runs = tq // run
    n_res = min(dilation, ATT_ROWS // tq)
    n_tiles = seq // PERM_ROWS
    view = proj.reshape(bsz, n_tiles, dilation, run, PROJ_W)
    n_groups = len(A_GROUPS)
    cq = COL_A // A_WIDTH + g
    ck = cq + n_groups
    cv = ck + n_groups
    halo_per_run = run // A_HALF
    n_halo = sub // A_HALF
    halo_per_tq = tq // A_HALF

    def main(width, c):
        return pl.BlockSpec((None, n_runs, n_res, run, width), lambda b, r, i: (b, i, r, 0, c))

    def halo(c, which):
        def index(b, r, i):
            if which == "prev":
                hb = jnp.maximum(i * halo_per_tq - 1, 0)
            else:
                hb = jnp.minimum((i + 1) * halo_per_tq, n_halo - 1)
            return (b, hb // halo_per_run, r, hb % halo_per_run, c)
        return pl.BlockSpec((None, None, n_res, A_HALF, A_WIDTH), index)

    o, st = pl.pallas_call(
        functools.partial(_attn_kernel, tq=tq, sub=sub, run=run, n_res=n_res),
        grid=(bsz, dilation // n_res, sub // tq),
        in_specs=[main(A_WIDTH, cq), halo(ck, "prev"), main(A_WIDTH, ck), halo(ck, "next"),
                  halo(cv, "prev"), main(A_WIDTH, cv), halo(cv, "next"),
                  _resident((4 * A_HEADS, ATT_SB, ATT_KW))],
        out_specs=[main(A_WIDTH, 0), main(LANES, 0)],
        out_shape=[jax.ShapeDtypeStruct((bsz, n_tiles, dilation, run, A_WIDTH), BF16),
                   jax.ShapeDtypeStruct((bsz, n_tiles, dilation, run, LANES), F32)],
        compiler_params=_cparams("parallel", "parallel", "arbitrary"),
        name=f"attn_d{dilation}",
    )(view, view, view, view, view, view, view, bm)
    n_perm = bsz * n_tiles
    return (o.reshape(n_perm, dilation, run, A_WIDTH), st.reshape(n_perm, dilation, run, LANES))


def _gla_kernel(q_ref, k_ref, v_ref, r_ref, gl_ref, wgf_ref, wgb_ref, bf_ref, bb_ref, gn_ref,
                o_ref, oacc, sf, sb, *, seq):
    c = GLA_ROWS
    half = c // 2
    n_blocks = seq // c
    u = min(GLA_UNROLL, n_blocks // 2)
    sf[...] = jnp.zeros_like(sf)
    sb[...] = jnp.zeros_like(sb)
    ri = lax.broadcasted_iota(jnp.int32, (c, c), 0)
    ci = lax.broadcasted_iota(jnp.int32, (c, c), 1)
    tril = (ci <= ri).astype(BF16)
    tril2 = jnp.concatenate([tril, tril], axis=1)
    nt = (((1,), (1,)), ((), ()))

    def log2_decay(z):
        e = jnp.exp2(jnp.abs(z) * (-LOG2E))
        return (jnp.minimum(z, 0.0) - jnp.log(1.0 + e)) * (LOG2E / B_GATE_TAU)

    def prefix_sum(la):
        hi = la.astype(BF16)
        lo = (la - hi.astype(F32)).astype(BF16)
        return jnp.dot(tril2, jnp.concatenate([hi, lo], axis=0), preferred_element_type=F32)

    def exponents(la, cs, forward):
        tot = cs[c - 1:c, :]
        if forward:
            ref = cs[half - 1:half, :]
            return cs - ref, ref, tot
        suf = tot - cs + la
        ref = suf[half:half + 1, :]
        return suf - ref, ref, tot

    def body(nb, finish):
        rows_f = [pl.multiple_of((nb * u + j) * c, c) for j in range(u)]
        rows_b = [pl.multiple_of((n_blocks - 1 - nb * u - j) * c, c) for j in range(u)]
        chains = [(r, True) for r in rows_f] + [(r, False) for r in rows_b]
        zs = [jnp.dot(gl_ref[0, pl.ds(r, c), :], (wgf_ref if fw else wgb_ref)[0],
                      preferred_element_type=F32) + (bf_ref if fw else bb_ref)[0]
              for r, fw in chains]
        las = [log2_decay(z) for z in zs]
        css = [prefix_sum(la) for la in las]
        qes, kes, qis, kds, decs = [], [], [], [], []
        for (r, fw), la, cs in zip(chains, las, css):
            rel, ref, tot = exponents(la, cs, fw)
            qp = q_ref[0, pl.ds(r, c), :].astype(F32) * jnp.exp2(rel)
            kp = k_ref[0, pl.ds(r, c), :].astype(F32) * jnp.exp2(-rel)
            qes.append(qp.astype(BF16))
            kes.append(kp.astype(BF16))
            qis.append((qp * jnp.exp2(ref)).astype(BF16))
            kds.append(kp * jnp.exp2(tot - ref))
            decs.append(jnp.exp2(tot))
        atts = [lax.dot_general(qe, ke, nt, preferred_element_type=F32) for qe, ke in zip(qes, kes)]
        atts = [jnp.where((ci <= ri) if fw else (ci > ri), a, 0.0).astype(BF16)
                for (_, fw), a in zip(chains, atts)]
        vs = [v_ref[0, pl.ds(r, c), :] for r, _ in chains]
        ups = [jnp.dot(kd.T.astype(BF16), v, preferred_element_type=F32) for kd, v in zip(kds, vs)]
        states = []
        for st_ref, first in ((sf, 0), (sb, u)):
            st = st_ref[...]
            for idx in range(first, first + u):
                states.append(st.astype(BF16))
                dcol = jnp.broadcast_to(decs[idx], (B_DK, B_DK)).T
                st = st * jnp.concatenate([dcol, dcol], axis=1) + ups[idx]
            st_ref[...] = st
        outs = [jnp.dot(jnp.concatenate([qi, a], axis=1), jnp.concatenate([st, v], axis=0),
                        preferred_element_type=F32)
                for a, v, qi, st in zip(atts, vs, qis, states)]
        for (r, _), o in zip(chains, outs):
            if finish:
                y = _rms(o + oacc[pl.ds(r, c), :], gn_ref[...])
                rr = r_ref[0, pl.ds(r, c), :].astype(F32)
                o_ref[0, pl.ds(r, c), :] = (y * (rr * _sigmoid(rr))).astype(BF16)
            else:
                oacc[pl.ds(r, c), :] = o

    def first_half(nb, carry):
        body(nb, False)
        return carry

    def second_half(nb, carry):
        body(nb, True)
        return carry

    n_bodies = n_blocks // u
    lax.fori_loop(0, n_bodies // 2, first_half, 0)
    lax.fori_loop(n_bodies // 2, n_bodies, second_half, 0)


def _gla(proj, bsz, seq, wgf, wgb, bgf, bgb, gn):
    view = proj.reshape(bsz, seq, PROJ_W)

    def cols(width, col0):
        return pl.BlockSpec((1, seq, width), lambda b, h: (b, 0, col0 // width + h))

    def per_head(shape):
        return pl.BlockSpec((1,) + shape, lambda b, h: (h, 0, 0))

    out = pl.pallas_call(
        functools.partial(_gla_kernel, seq=seq),
        grid=(bsz, B_HEADS),
        in_specs=[cols(B_DK, COL_QB), cols(B_DK, COL_KB), cols(B_DV, COL_VB), cols(B_DV, COL_RB),
                  pl.BlockSpec((1, seq, LANES), lambda b, h: (b, 0, COL_GL // LANES)),
                  per_head((LANES, B_DK)), per_head((LANES, B_DK)),
                  per_head((1, B_DK)), per_head((1, B_DK)),
                  pl.BlockSpec((1, B_DV), lambda b, h: (0, 0))],
        out_specs=pl.BlockSpec((1, seq, B_DV), lambda b, h: (b, 0, h)),
        out_shape=jax.ShapeDtypeStruct((bsz, seq, B_VW), BF16),
        scratch_shapes=[pltpu.VMEM((seq, B_DV), F32),
                        pltpu.VMEM((B_DK, B_DV), F32),
                        pltpu.VMEM((B_DK, B_DV), F32)],
        compiler_params=_cparams("parallel", "arbitrary"),
        name="gla",
    )(view, view, view, view, view, wgf, wgb, bgf, bgb, gn)
    return out.reshape(bsz * seq, B_VW)


def _merge_rows(o1_ref, o2_ref, o3_ref, s1_ref, s2_ref, s3_ref, ob_ref, ga_ref, gb_ref, x_ref,
                e_ref, wa_ref, wb_ref, wo_ref, on2, on3, sn2, sn3):
    n_slabs = A_WIDTH // LANES
    for (_, d), o_ref, s_ref, on, sn in ((A_GROUPS[1], o2_ref, s2_ref, on2, sn2),
                                        (A_GROUPS[2], o3_ref, s3_ref, on3, sn3)):
        n = ROW_TM // d
        for r in range(d):
            sn[pl.ds(r, n, stride=d), :] = s_ref[r]
            for s in range(n_slabs):
                on[s, pl.ds(r, n, stride=d), :] = o_ref[r, :, s * LANES:(s + 1) * LANES].astype(F32)

    ms = (s1_ref[...], sn2[...], sn3[...])
    ls = [pltpu.roll(m, LANES - A_HEADS, axis=1) for m in ms]
    top = jnp.maximum(jnp.maximum(ms[0], ms[1]), ms[2])
    es = [jnp.exp2(m - top) for m in ms]
    inv = 1.0 / (es[0] * ls[0] + es[1] * ls[1] + es[2] * ls[2])
    ws = [jnp.dot((e * inv).astype(BF16), e_ref[...], preferred_element_type=F32) for e in es]
    slabs = []
    for s in range(n_slabs):
        sl = slice(s * LANES, (s + 1) * LANES)
        slabs.append(o1_ref[:, sl].astype(F32) * ws[0][:, sl]
                     + on2[s] * ws[1][:, sl] + on3[s] * ws[2][:, sl])
    oa = jnp.concatenate(slabs, axis=1)
    ya = jnp.dot(oa.astype(BF16), wa_ref[...], preferred_element_type=F32)
    yb = jnp.dot(ob_ref[...], wb_ref[...], preferred_element_type=F32)
    merged = _sigmoid(ga_ref[...].astype(F32)) * ya + _sigmoid(gb_ref[...].astype(F32)) * yb
    return x_ref[...] + jnp.dot(merged.astype(BF16), wo_ref[...], preferred_element_type=F32)


def _mlp_rows(x, g_ref, w1_ref, w2_ref):
    h = (x * g_ref[...]).astype(BF16)
    acc = jnp.zeros_like(x)
    for c in range(D_FF // D_MODEL):
        sl = slice(c * D_MODEL, (c + 1) * D_MODEL)
        u = jnp.maximum(jnp.dot(h, w1_ref[:, sl], preferred_element_type=F32), 0.0)
        acc = acc + jnp.dot((u * u).astype(BF16), w2_ref[sl, :], preferred_element_type=F32)
    return x + acc * (1.0 / (jnp.mean(x * x, axis=-1, keepdims=True) + EPS))


def _tail_kernel(o1_ref, o2_ref, o3_ref, s1_ref, s2_ref, s3_ref, ob_ref, ga_ref, gb_ref, x_ref,
                 e_ref, wa_ref, wb_ref, wo_ref, g_ref, w1_ref, w2_ref, fg_ref, out_ref,
                 on2, on3, sn2, sn3, *, final):
    x1 = _merge_rows(o1_ref, o2_ref, o3_ref, s1_ref, s2_ref, s3_ref, ob_ref, ga_ref, gb_ref, x_ref,
                     e_ref, wa_ref, wb_ref, wo_ref, on2, on3, sn2, sn3)
    x2 = _mlp_rows(x1, g_ref, w1_ref, w2_ref)
    out_ref[...] = _rms(x2, fg_ref[...]) if final else x2


def _tail(oas, sts, ob, proj, x, expand, wa, wb, wo, g, w1, w2, fg, final):
    t = x.shape[0]
    per_perm = PERM_ROWS // ROW_TM

    def rows(width, cblk=0):
        return pl.BlockSpec((ROW_TM, width), lambda i: (i, cblk))

    def perm(width, d):
        return pl.BlockSpec((None, d, ROW_TM // d, width), lambda i: (i // per_perm, 0, i % per_perm, 0))

    d2, d3 = A_GROUPS[1][1], A_GROUPS[2][1]
    return pl.pallas_call(
        functools.partial(_tail_kernel, final=final),
        grid=(t // ROW_TM,),
        in_specs=[rows(A_WIDTH), perm(A_WIDTH, d2), perm(A_WIDTH, d3),
                  rows(LANES), perm(LANES, d2), perm(LANES, d3),
                  rows(B_VW), rows(D_MODEL, COL_GA // D_MODEL), rows(D_MODEL, COL_GB // D_MODEL),
                  rows(D_MODEL),
                  _resident((LANES, A_WIDTH)), _resident((A_WIDTH, D_MODEL)),
                  _resident((B_VW, D_MODEL)), _resident((D_MODEL, D_MODEL)),
                  _resident((1, D_MODEL)), _resident((D_MODEL, D_FF)), _resident((D_FF, D_MODEL)),
                  _resident((1, D_MODEL))],
        out_specs=rows(D_MODEL),
        out_shape=jax.ShapeDtypeStruct((t, D_MODEL), F32),
        scratch_shapes=[pltpu.VMEM((A_WIDTH // LANES, ROW_TM, LANES), F32),
                        pltpu.VMEM((A_WIDTH // LANES, ROW_TM, LANES), F32),
                        pltpu.VMEM((ROW_TM, LANES), F32),
                        pltpu.VMEM((ROW_TM, LANES), F32)],
        compiler_params=_cparams("parallel"),
        name="tail_final" if final else "tail",
    )(oas[0].reshape(t, A_WIDTH), oas[1], oas[2], sts[0].reshape(t, LANES), sts[1], sts[2],
      ob, proj, proj, x, expand, wa, wb, wo, g, w1, w2, fg)


def _prep_layer(l, norm_mix_g, w_in, wgf, bgf, wgb, bgb, gla_norm_g, w_branch_a, w_branch_b, w_out,
                norm_ffn_g, w_ff1, w_ff2):
    w = w_in[l]
    n_groups = len(A_GROUPS)
    a_end = 3 * n_groups * A_WIDTH
    gl1 = a_end + 2 * B_KW + 2 * B_VW + 2 * B_GATE_RANK
    col = jnp.arange(gl1)
    scale = jnp.where(col < n_groups * A_WIDTH, A_HEAD_DIM ** -0.5 * LOG2E,
                      jnp.where((col >= a_end) & (col < a_end + B_KW), B_DK ** -0.5, 1.0)).astype(w.dtype)
    w_fused = jnp.concatenate(
        [(w[:, :gl1] * scale[None, :]).astype(BF16), jnp.zeros((D_MODEL, GL_PAD - 2 * B_GATE_RANK), BF16),
         w[:, gl1:].astype(BF16)], axis=1)

    def gate_map(wg, row0):
        per_head = wg.reshape(B_GATE_RANK, B_HEADS, B_DK).transpose(1, 0, 2)
        return jnp.pad(per_head, ((0, 0), (row0, LANES - row0 - B_GATE_RANK), (0, 0))).astype(BF16)

    return dict(
        g_mix=norm_mix_g[l].reshape(1, D_MODEL),
        w_in=w_fused,
        wgf=gate_map(wgf[l], 0),
        wgb=gate_map(wgb[l], B_GATE_RANK),
        bgf=bgf[l].reshape(B_HEADS, 1, B_DK),
        bgb=bgb[l].reshape(B_HEADS, 1, B_DK),
        gn=gla_norm_g[l].reshape(1, B_DV),
        wa=w_branch_a[l].astype(BF16),
        wb=w_branch_b[l].astype(BF16),
        wo=w_out[l].astype(BF16),
        g_ffn=norm_ffn_g[l].reshape(1, D_MODEL),
        w1=w_ff1[l].astype(BF16),
        w2=w_ff2[l].astype(BF16),
    )


def _trunk(x, layers, biases, expand, final_g):
    bsz, seq, _ = x.shape
    xf = x.reshape(bsz * seq, D_MODEL)
    for li, p in enumerate(layers):
        proj = _inproj(xf, p["g_mix"], p["w_in"])
        oas, sts = [], []
        for g, (_, dilation) in enumerate(A_GROUPS):
            o, st = _attention_group(proj, bsz, seq, g, dilation, biases[g])
            oas.append(o)
            sts.append(st)
        ob = _gla(proj, bsz, seq, p["wgf"], p["wgb"], p["bgf"], p["bgb"], p["gn"])
        xf = _tail(oas, sts, ob, proj, xf, expand, p["wa"], p["wb"], p["wo"],
                   p["g_ffn"], p["w1"], p["w2"], final_g, final=(li == len(layers) - 1))
    return xf.reshape(bsz, seq, D_MODEL)


def kernel(x_prompt, x_sample, norm_mix_g, w_in, gla_w_gate_fwd, gla_b_gate_fwd, gla_w_gate_bwd,
           gla_b_gate_bwd, gla_norm_g, w_branch_a, w_branch_b, w_out, norm_ffn_g, w_ff1, w_ff2,
           final_norm_g):
    layers = [_prep_layer(l, norm_mix_g, w_in, gla_w_gate_fwd, gla_b_gate_fwd, gla_w_gate_bwd,
                          gla_b_gate_bwd, gla_norm_g, w_branch_a, w_branch_b, w_out, norm_ffn_g,
                          w_ff1, w_ff2) for l in range(DEPTH)]
    n_heads = len(A_GROUPS) * A_HEADS
    slopes = jnp.exp2(-8.0 * jnp.arange(1, n_heads + 1, dtype=F32) / n_heads)
    biases = [_attn_bias(slopes[g * A_HEADS:(g + 1) * A_HEADS], dilation)
              for g, (_, dilation) in enumerate(A_GROUPS)]
    expand = (jnp.arange(LANES)[:, None] == jnp.arange(A_WIDTH)[None, :] // A_HEAD_DIM).astype(BF16)
    final_g = final_norm_g.reshape(1, D_MODEL)
    return (_trunk(x_prompt, layers, biases, expand, final_g),
            _trunk(x_sample, layers, biases, expand, final_g))
```
